```python
import jax
import jax.numpy as jnp
from jax import lax
import numpy as np

D_MODEL = 1024
BATCH = 16
SEQ = 2048
DEPTH = 4
DEC_BATCH = 8
DEC_SEQ = 32
PAST_LEN = 1024

CHUNK = 64
HEAD_DIM = 64
HQ_A = 8
HKV_A = 2
G_A = HQ_A // HKV_A
WINDOW_A = 128
NPREV_A = WINDOW_A // CHUNK
H_B = 4
DH_B = 128
CONV_W = 4
H_C = 16
NPREV_C = 8
MAX_REL_C = 256
D_FF = ((8 * D_MODEL + 3 * 256 - 1) // (3 * 256)) * 256
EPS = 1e-6
NEG = -1e30
N_AB = (DEPTH + 1) // 2
N_C = DEPTH // 2
W_A = HQ_A * HEAD_DIM
KV_A = HKV_A * HEAD_DIM
W_B = H_B * DH_B
W_C = H_C * HEAD_DIM
SPLIT_AB = [W_A, W_A + KV_A, W_A + 2 * KV_A, W_A + 2 * KV_A + 2 * W_B, W_A + 2 * KV_A + 3 * W_B,
            W_A + 2 * KV_A + 3 * W_B + H_B, W_A + 2 * KV_A + 3 * W_B + 2 * H_B]
P_AB = W_A + 2 * KV_A + 3 * W_B + 2 * H_B + W_B

kernel_name = 'hybrid_streaming_encoder_step'


def rmsnorm(x, g):
    xf = x.astype(jnp.float32)
    y = xf * lax.rsqrt(jnp.mean(xf * xf, axis=-1, keepdims=True) + EPS)
    return (y * g.astype(jnp.float32)).astype(x.dtype)


def alibi_slopes():
    h = jnp.arange(1, HQ_A + 1, dtype=jnp.float32)
    return jnp.exp2(-8.0 * h / HQ_A).reshape(HKV_A, G_A)


def alibi_bias(slopes, q_pos, k_pos):
    dist = jnp.abs(q_pos[:, None] - k_pos[None, :]).astype(jnp.float32)
    return -slopes[:, :, None, None] * dist


def relpos_bias(table, q_pos, k_pos):
    rel = jnp.clip(q_pos[:, None] - k_pos[None, :], -MAX_REL_C, MAX_REL_C) + MAX_REL_C
    return table.astype(jnp.float32)[:, rel][:, None]


def attend(q, k, v, bias, valid, sinks):
    B, Lq, Hkv, G, d = q.shape
    s = jnp.einsum('bqhgd,bkhd->bhgqk', q, k).astype(jnp.float32) * (d ** -0.5) + bias
    if valid is not None:
        s = jnp.where(valid, s, NEG)
    if sinks is None:
        p = jax.nn.softmax(s, axis=-1)
    else:
        sk = sinks.astype(jnp.float32)[None, :, :, None]
        m = jnp.maximum(s.max(axis=-1), sk)
        p = jnp.exp(s - m[..., None])
        p = p / (p.sum(axis=-1, keepdims=True) + jnp.exp(sk - m)[..., None])
    o = jnp.einsum('bhgqk,bkhd->bqhgd', p.astype(v.dtype), v)
    return o.reshape(B, Lq, Hkv * G * d)


def band_attention_prompt(q, k, v, n_prev, bias_fn, sinks):
    B, S = q.shape[:2]
    nc = S // CHUNK
    pad = n_prev * CHUNK
    band = pad + CHUNK
    kp = jnp.pad(k, ((0, 0), (pad, 0), (0, 0), (0, 0)))
    vp = jnp.pad(v, ((0, 0), (pad, 0), (0, 0), (0, 0)))
    qc = jnp.moveaxis(q.reshape((B, nc, CHUNK) + q.shape[2:]), 1, 0)

    def one_chunk(args):
        c, q_c = args
        start = c * CHUNK
        k_b = lax.dynamic_slice_in_dim(kp, start, band, axis=1)
        v_b = lax.dynamic_slice_in_dim(vp, start, band, axis=1)
        q_pos = start + jnp.arange(CHUNK)
        k_pos = start - pad + jnp.arange(band)
        valid = (k_pos >= 0)[None, :]
        return attend(q_c, k_b, v_b, bias_fn(q_pos, k_pos), valid, sinks)

    out = lax.map(one_chunk, (jnp.arange(nc), qc))
    return jnp.moveaxis(out, 0, 1).reshape(B, S, -1)


def causal_conv(u, buf, w, b):
    T = u.shape[1]
    up = jnp.concatenate([buf.astype(u.dtype), u], axis=1)
    y = up[:, 0:T] * w[0]
    for j in range(1, CONV_W):
        y = y + up[:, j:j + T] * w[j]
    return jax.nn.silu(y + b), up[:, -(CONV_W - 1):]


def mlstm_chunk(state, q, k, v, logi, logf):
    C, n, m = state
    L = q.shape[2]
    b = jnp.cumsum(logf, axis=-1)
    causal = jnp.tril(jnp.ones((L, L), dtype=bool))
    logD = jnp.where(causal, b[..., :, None] - b[..., None, :] + logi[..., None, :], -jnp.inf)
    inter = b + m[..., None]
    m_t = jnp.maximum(inter, logD.max(axis=-1))
    a = jnp.exp(inter - m_t)
    w = jnp.einsum('bhtd,bhsd->bhts', q, k) * jnp.exp(logD - m_t[..., None])
    num = a[..., None] * jnp.einsum('bhvk,bhtk->bhtv', C, q) + jnp.einsum('bhts,bhsv->bhtv', w, v)
    den = a * jnp.einsum('bhk,bhtk->bht', n, q) + w.sum(axis=-1)
    h = num / jnp.maximum(jnp.abs(den), jnp.exp(-m_t))[..., None]
    bL = b[..., -1]
    logw_end = bL[..., None] - b + logi
    m_new = jnp.maximum(bL + m, logw_end.max(axis=-1))
    a_end = jnp.exp(bL + m - m_new)
    w_end = jnp.exp(logw_end - m_new[..., None])
    C_new = a_end[..., None, None] * C + jnp.einsum('bhs,bhsv,bhsk->bhvk', w_end, v, k)
    n_new = a_end[..., None] * n + jnp.einsum('bhs,bhsk->bhk', w_end, k)
    return (C_new, n_new, m_new), h


def mixer_ab(h, j, W, cache):
    B, T, _ = h.shape
    f32 = jnp.float32
    qa, ka, va, qkb, vb, ig, fg, og = jnp.split(h @ W['w_in_ab'][j], SPLIT_AB, axis=-1)
    qa = qa.reshape(B, T, HKV_A, G_A, HEAD_DIM)
    ka = ka.reshape(B, T, HKV_A, HEAD_DIM)
    va = va.reshape(B, T, HKV_A, HEAD_DIM)
    slopes = alibi_slopes()
    sinks = W['sink_a'][j].reshape(HKV_A, G_A)
    bias_fn = lambda qp, kp: alibi_bias(slopes, qp, kp)
    if cache is None:
        out_a = band_attention_prompt(qa, ka, va, NPREV_A, bias_fn, sinks)
        new_k, new_v = ka[:, -WINDOW_A:], va[:, -WINDOW_A:]
        conv_buf = jnp.zeros((B, CONV_W - 1, 2 * W_B), h.dtype)
        state = (jnp.zeros((B, H_B, DH_B, DH_B), f32), jnp.zeros((B, H_B, DH_B), f32), jnp.zeros((B, H_B), f32))
    else:
        ck, cv, s_c, s_n, s_m, conv_buf = cache
        lc = ck.shape[1]
        k_all = jnp.concatenate([ck.astype(ka.dtype), ka], axis=1)
        v_all = jnp.concatenate([cv.astype(va.dtype), va], axis=1)
        q_pos = PAST_LEN + jnp.arange(T)
        k_pos = PAST_LEN - lc + jnp.arange(lc + T)
        out_a = attend(qa, k_all, v_all, bias_fn(q_pos, k_pos), None, sinks)
        new_k, new_v = ka, va
        state = (s_c.astype(f32), s_n.astype(f32), s_m.astype(f32))
    qk_c, conv_new = causal_conv(qkb, conv_buf, W['conv_w_b'][j], W['conv_b_b'][j])
    heads = lambda a: a.reshape(B, T, H_B, DH_B).transpose(0, 2, 1, 3).astype(f32)
    qb = heads(qk_c[..., :W_B])
    kb = heads(qk_c[..., W_B:]) * (DH_B ** -0.5)
    vh = heads(vb)
    bg = W['b_gates_b'][j].astype(f32)
    logi = (ig.astype(f32) + bg[:H_B]).transpose(0, 2, 1)
    logf = jax.nn.log_sigmoid(fg.astype(f32) + bg[H_B:]).transpose(0, 2, 1)
    if cache is None:
        nc = T // CHUNK
        to_chunks = lambda a: jnp.moveaxis(a.reshape(a.shape[:2] + (nc, CHUNK) + a.shape[3:]), 2, 0)
        xs = tuple(to_chunks(a) for a in (qb, kb, vh, logi, logf))
        state, hs = lax.scan(lambda st, x_c: mlstm_chunk(st, *x_c), state, xs)
        hb = jnp.moveaxis(hs, 0, 2).reshape(B, H_B, T, DH_B)
    else:
        state, hb = mlstm_chunk(state, qb, kb, vh, logi, logf)
    hb = hb.transpose(0, 2, 1, 3)
    hb = hb * lax.rsqrt(jnp.mean(hb * hb, axis=-1, keepdims=True) + EPS) * W['hnorm_b'][j].astype(f32).reshape(H_B, DH_B)
    hb = jax.nn.sigmoid(og.astype(f32)) * hb.reshape(B, T, W_B)
    out = jnp.concatenate([out_a, hb.astype(h.dtype)], axis=-1) @ W['w_out_ab'][j]
    return out, (new_k, new_v) + state + (conv_new,)


def mixer_c(h, j, W, cache):
    B, T, _ = h.shape
    q, k, v = jnp.split(h @ W['w_in_c'][j], 3, axis=-1)
    q = q.reshape(B, T, H_C, 1, HEAD_DIM)
    k = k.reshape(B, T, H_C, HEAD_DIM)
    v = v.reshape(B, T, H_C, HEAD_DIM)
    table = W['relbias_c'][j]
    bias_fn = lambda qp, kp: relpos_bias(table, qp, kp)
    if cache is None:
        out = band_attention_prompt(q, k, v, NPREV_C, bias_fn, None)
        new_k, new_v = k[:, -NPREV_C * CHUNK:], v[:, -NPREV_C * CHUNK:]
    else:
        ck, cv = cache
        lc = ck.shape[1]
        k_all = jnp.concatenate([ck.astype(k.dtype), k], axis=1)
        v_all = jnp.concatenate([cv.astype(v.dtype), v], axis=1)
        q_pos = PAST_LEN + jnp.arange(T)
        k_pos = PAST_LEN - lc + jnp.arange(lc + T)
        out = attend(q, k_all, v_all, bias_fn(q_pos, k_pos), None, None)
        new_k, new_v = k, v
    return out @ W['w_out_c'][j], (new_k, new_v)


def swiglu(h, i, W):
    return (jax.nn.silu(h @ W['w_ffn_gate'][i]) * (h @ W['w_ffn_up'][i])) @ W['w_ffn_down'][i]


def trunk(x, c, caches, W):
    ab_states, c_states = [], []
    for i in range(DEPTH):
        mod = jax.nn.silu(c) @ W['w_ada'][i] + W['b_ada'][i]
        sh1, sc1, gt1, sh2, sc2, gt2 = [m[:, None, :] for m in jnp.split(mod, 6, axis=-1)]
        g = W['norm_g'][i]
        h = rmsnorm(x, g[0]) * (1 + sc1) + sh1
        j = i // 2
        if i % 2 == 0:
            cache = None if caches is None else tuple(a[j] for a in caches[:6])
            y, st = mixer_ab(h, j, W, cache)
            ab_states.append(st)
        else:
            cache = None if caches is None else tuple(a[j] for a in caches[6:])
            y, st = mixer_c(h, j, W, cache)
            c_states.append(st)
        x = x + gt1 * rmsnorm(y, g[1])
        h = rmsnorm(x, g[2]) * (1 + sc2) + sh2
        x = x + gt2 * rmsnorm(swiglu(h, i, W), g[3])
    stack = lambda sts: tuple(jnp.stack(s) for s in zip(*sts))
    return x, stack(ab_states) + stack(c_states)


def setup_inputs(seed: int = 0) -> dict:
    key = jax.random.key(seed)
    keys = iter(jax.random.split(key, 40))

    def nrm(shape, scale):
        return jax.random.normal(next(keys), shape, jnp.float32) * scale

    d = D_MODEL
    la = min(WINDOW_A, PAST_LEN)
    lc = min(NPREV_C * CHUNK, PAST_LEN)
    return {
        'x_prompt': nrm((BATCH, SEQ, d), 1.0),
        'x_sample': nrm((DEC_BATCH, DEC_SEQ, d), 1.0),
        'cache_a_k': nrm((N_AB, DEC_BATCH, la, HKV_A, HEAD_DIM), 1.0),
        'cache_a_v': nrm((N_AB, DEC_BATCH, la, HKV_A, HEAD_DIM), 1.0),
        'state_b_c': nrm((N_AB, DEC_BATCH, H_B, DH_B, DH_B), 0.3),
        'state_b_n': nrm((N_AB, DEC_BATCH, H_B, DH_B), 0.3),
        'state_b_m': nrm((N_AB, DEC_BATCH, H_B), 0.5),
        'state_b_conv': nrm((N_AB, DEC_BATCH, CONV_W - 1, 2 * W_B), 1.0),
        'cache_c_k': nrm((N_C, DEC_BATCH, lc, H_C, HEAD_DIM), 1.0),
        'cache_c_v': nrm((N_C, DEC_BATCH, lc, H_C, HEAD_DIM), 1.0),
        'c_prompt': nrm((BATCH, d), 1.0),
        'c_sample': nrm((DEC_BATCH, d), 1.0),
        'w_in_ab': nrm((N_AB, d, P_AB), d ** -0.5),
        'sink_a': nrm((N_AB, HQ_A), 0.5),
        'conv_w_b': nrm((N_AB, CONV_W, 2 * W_B), CONV_W ** -0.5),
        'conv_b_b': nrm((N_AB, 2 * W_B), 0.02),
        'b_gates_b': jnp.concatenate([nrm((N_AB, H_B), 0.1), 3.0 + nrm((N_AB, H_B), 0.5)], axis=-1),
        'hnorm_b': 1.0 + nrm((N_AB, W_B), 0.1),
        'w_out_ab': nrm((N_AB, W_A + W_B, d), (W_A + W_B) ** -0.5),
        'w_in_c': nrm((N_C, d, 3 * W_C), d ** -0.5),
        'relbias_c': nrm((N_C, H_C, 2 * MAX_REL_C + 1), 0.5),
        'w_out_c': nrm((N_C, W_C, d), W_C ** -0.5),
        'w_ada': nrm((DEPTH, d, 6 * d), 0.5 * d ** -0.5),
        'b_ada': nrm((DEPTH, 6 * d), 0.02),
        'norm_g': 1.0 + nrm((DEPTH, 4, d), 0.1),
        'w_ffn_gate': nrm((DEPTH, d, D_FF), d ** -0.5),
        'w_ffn_up': nrm((DEPTH, d, D_FF), d ** -0.5),
        'w_ffn_down': nrm((DEPTH, D_FF, d), D_FF ** -0.5),
    }


def reference(x_prompt, x_sample, cache_a_k, cache_a_v, state_b_c, state_b_n, state_b_m, state_b_conv,
              cache_c_k, cache_c_v, c_prompt, c_sample, w_in_ab, sink_a, conv_w_b, conv_b_b, b_gates_b,
              hnorm_b, w_out_ab, w_in_c, relbias_c, w_out_c, w_ada, b_ada, norm_g, w_ffn_gate, w_ffn_up,
              w_ffn_down):
    W = dict(w_in_ab=w_in_ab, sink_a=sink_a, conv_w_b=conv_w_b, conv_b_b=conv_b_b, b_gates_b=b_gates_b,
             hnorm_b=hnorm_b, w_out_ab=w_out_ab, w_in_c=w_in_c, relbias_c=relbias_c, w_out_c=w_out_c,
             w_ada=w_ada, b_ada=b_ada, norm_g=norm_g, w_ffn_gate=w_ffn_gate, w_ffn_up=w_ffn_up,
             w_ffn_down=w_ffn_down)
    y_prompt, sp = trunk(x_prompt, c_prompt, None, W)
    caches = (cache_a_k, cache_a_v, state_b_c, state_b_n, state_b_m, state_b_conv, cache_c_k, cache_c_v)
    y_sample, ss = trunk(x_sample, c_sample, caches, W)
    pa_k, pa_v, pb_c, pb_n, pb_m, pb_conv, pc_k, pc_v = sp
    sa_k, sa_v, sb_c, sb_n, sb_m, sb_conv, sc_k, sc_v = ss
    return (y_prompt, y_sample, pa_k, pa_v, pb_c, pb_n, pb_m, pb_conv, pc_k, pc_v,
            sa_k, sa_v, sb_c, sb_n, sb_m, sb_conv, sc_k, sc_v)
```

```python
import functools

import jax
import jax.numpy as jnp
from jax import lax
from jax.experimental import pallas as pl
from jax.experimental.pallas import tpu as pltpu

F32 = jnp.float32
BF16 = jnp.bfloat16

D_MODEL = 1024
CHUNK = 64
HEAD_DIM = 64
HQ_A = 8
HKV_A = 2
WINDOW_A = 128
H_B = 4
DH_B = 128
CONV_W = 4
H_C = 16
NPREV_C = 8
MAX_REL_C = 256
EPS = 1e-6
NEG = -1e30
W_A = HQ_A * HEAD_DIM
KV_A = HKV_A * HEAD_DIM
W_B = H_B * DH_B
W_C = H_C * HEAD_DIM
BAND_A = WINDOW_A + CHUNK
BAND_C = (NPREV_C + 1) * CHUNK
FF_CHUNK = 256
GATE_W = 128
TAIL_ROWS = 8
ROW_TILE = 512
V7X_VMEM_LIMIT = 56 * 1024 * 1024


def _params(sem, vmem=V7X_VMEM_LIMIT):
    return pltpu.CompilerParams(dimension_semantics=sem, vmem_limit_bytes=vmem)


def _const_spec(shape):
    nd = len(shape)
    return pl.BlockSpec(shape, lambda *_: (0,) * nd, pipeline_mode=pl.Buffered(1))


def _split3(x):
    hi = x.astype(BF16)
    r1 = x - hi.astype(F32)
    mid = r1.astype(BF16)
    lo = (r1 - mid.astype(F32)).astype(BF16)
    return hi, mid, lo


def _dot(a, b):
    return jnp.dot(a, b, preferred_element_type=F32)


def _dot_nt(a, b):
    return lax.dot_general(a, b, (((1,), (1,)), ((), ())), preferred_element_type=F32)


def _rms(x, g):
    return x * lax.rsqrt(jnp.mean(x * x, axis=-1, keepdims=True) + EPS) * g


def _sigmoid(x):
    return 1.0 / (1.0 + jnp.exp(-x))


def _log_sigmoid(x):
    return jnp.minimum(x, 0.0) - jnp.log(1.0 + jnp.exp(-jnp.abs(x)))


def _ada_kernel(c_ref, w_ref, b_ref, o_ref):
    c = c_ref[...]
    s = c * _sigmoid(c)
    s_hi, s_mid, _ = _split3(s)
    w_hi, w_mid, _ = _split3(w_ref[0])
    acc = _dot(s_hi, w_hi) + _dot(s_hi, w_mid) + _dot(s_mid, w_hi)
    o_ref[0] = acc + b_ref[0]


def _ada_call(c_all, w_ada, b_ada):
    depth, d, n6 = w_ada.shape
    nb = c_all.shape[0]
    tn = 512
    return pl.pallas_call(
        _ada_kernel,
        grid=(depth, n6 // tn),
        in_specs=[
            pl.BlockSpec((nb, d), lambda i, j: (0, 0)),
            pl.BlockSpec((1, d, tn), lambda i, j: (i, 0, j)),
            pl.BlockSpec((1, 1, tn), lambda i, j: (i, 0, j)),
        ],
        out_specs=pl.BlockSpec((1, nb, tn), lambda i, j: (i, 0, j)),
        out_shape=jax.ShapeDtypeStruct((depth, nb, n6), F32),
        compiler_params=_params(("arbitrary", "arbitrary")),
        name="ada_mod",
    )(c_all, w_ada, b_ada.reshape(depth, 1, n6))


def _inproj_call(x, sc, sh, g, w, segs, tm, pad, name):
    B, L, d = x.shape
    nT = L // tm
    P = w.shape[1]
    tails = [s for s in segs if s[4]]
    n_seg = len(segs)

    def kernel(x_ref, sc_ref, sh_ref, g_ref, w_ref, *outs):
        seg_refs = outs[:n_seg]
        tail_refs = outs[n_seg:]
        t = pl.program_id(1)

        def compute():
            h = (_rms(x_ref[0], g_ref[...]) * (1.0 + sc_ref[0]) + sh_ref[0]).astype(BF16)
            ti = 0
            for (c0, wd, dt, _, tail_rows), r in zip(segs, seg_refs):
                y = _dot(h, w_ref[:, c0:c0 + wd])
                r[0] = y.astype(dt)
                if tail_rows:
                    tr = tail_refs[ti]
                    ti += 1

                    @pl.when(t == nT - 1 + pad)
                    def _():
                        tr[0] = y[tm - tail_rows:, :]

        if pad:
            @pl.when(t == 0)
            def _():
                for (_, wd, dt, padded, _), r in zip(segs, seg_refs):
                    if padded:
                        r[0] = jnp.zeros((tm, wd), dt)

            pl.when(t > 0)(compute)
        else:
            compute()

    def row_idx(t):
        return jnp.maximum(t - 1, 0) if pad else t

    in_specs = [
        pl.BlockSpec((1, tm, d), lambda b, t: (b, row_idx(t), 0)),
        pl.BlockSpec((1, 1, d), lambda b, t: (b, 0, 0)),
        pl.BlockSpec((1, 1, d), lambda b, t: (b, 0, 0)),
        _const_spec((1, d)),
        _const_spec((d, P)),
    ]
    out_specs, out_shapes = [], []
    for (_, wd, dt, padded, _) in segs:
        if padded:
            out_specs.append(pl.BlockSpec((1, tm, wd), lambda b, t: (b, t, 0)))
            out_shapes.append(jax.ShapeDtypeStruct((B, tm + L, wd), dt))
        else:
            out_specs.append(pl.BlockSpec((1, tm, wd), lambda b, t: (b, row_idx(t), 0)))
            out_shapes.append(jax.ShapeDtypeStruct((B, L, wd), dt))
    for (_, wd, _, _, tail_rows) in tails:
        out_specs.append(pl.BlockSpec((1, tail_rows, wd), lambda b, t: (b, 0, 0)))
        out_shapes.append(jax.ShapeDtypeStruct((B, tail_rows, wd), F32))

    return pl.pallas_call(
        kernel,
        grid=(B, nT + pad),
        in_specs=in_specs,
        out_specs=out_specs,
        out_shape=out_shapes,
        compiler_params=_params(("arbitrary", "arbitrary")),
        name=name,
    )(x, sc, sh, g, w)


def _attend(qh, pieces, sink):
    ss = []
    for k, _, bias, valid in pieces:
        s = _dot_nt(qh, k) + bias
        if valid is not None:
            s = jnp.where(valid, s, NEG)
        ss.append(s)
    m = ss[0].max(axis=-1, keepdims=True)
    for s in ss[1:]:
        m = jnp.maximum(m, s.max(axis=-1, keepdims=True))
    if sink is not None:
        m = jnp.maximum(m, sink)
    den = None
    o = None
    for s, (_, v, _, _) in zip(ss, pieces):
        p = jnp.exp(s - m)
        ps = p.sum(axis=-1, keepdims=True)
        pv = _dot(p.astype(BF16), v)
        den = ps if den is None else den + ps
        o = pv if o is None else o + pv
    if sink is not None:
        den = den + jnp.exp(sink - m)
    return o / den


def _alibi_slope(hq):
    return 2.0 ** (-8.0 * (hq + 1) / HQ_A)


def _band_attn_call(q, kv, extra, n_heads, n_kv, band, alibi, name):
    B, L, wq = q.shape
    nC = L // CHUNK
    grp = n_heads // n_kv
    hist = band - CHUNK
    v_off = n_kv * HEAD_DIM
    kvw = kv.shape[2]

    def kernel(q_ref, kv_ref, e_ref, o_ref):
        c = pl.program_id(1)
        start = pl.multiple_of(c * CHUNK + ROW_TILE - hist, CHUNK)
        jj = lax.broadcasted_iota(jnp.int32, (CHUNK, band), 1)
        valid = (c * CHUNK - hist + jj) >= 0
        if alibi:
            ii = lax.broadcasted_iota(jnp.int32, (CHUNK, band), 0)
            dist = jnp.abs(ii + hist - jj).astype(F32)
        q_all = q_ref[0] * (HEAD_DIM ** -0.5)
        for g in range(n_kv):
            kb = kv_ref[0, pl.ds(start, band), g * HEAD_DIM:(g + 1) * HEAD_DIM]
            vb = kv_ref[0, pl.ds(start, band), v_off + g * HEAD_DIM:v_off + (g + 1) * HEAD_DIM]
            for j in range(grp):
                hq = g * grp + j
                qh = q_all[:, hq * HEAD_DIM:(hq + 1) * HEAD_DIM]
                if alibi:
                    bias = dist * (-_alibi_slope(hq))
                    sink = e_ref[hq]
                else:
                    bias = e_ref[hq]
                    sink = None
                o = _attend(qh, [(kb, vb, bias, valid)], sink)
                o_ref[0, :, hq * HEAD_DIM:(hq + 1) * HEAD_DIM] = o.astype(BF16)

    if alibi:
        e_spec = pl.BlockSpec(memory_space=pltpu.SMEM)
    else:
        e_spec = _const_spec(extra.shape)
    return pl.pallas_call(
        kernel,
        grid=(B, nC),
        in_specs=[
            pl.BlockSpec((1, CHUNK, wq), lambda b, c: (b, c, 0)),
            pl.BlockSpec((1, ROW_TILE + L, kvw), lambda b, c: (b, 0, 0)),
            e_spec,
        ],
        out_specs=pl.BlockSpec((1, CHUNK, wq), lambda b, c: (b, c, 0)),
        out_shape=jax.ShapeDtypeStruct((B, L, wq), BF16),
        compiler_params=_params(("arbitrary", "arbitrary")),
        name=name,
    )(q, kv, extra)


def _cache_attn_call(q, kc, vc, kvn, extra, n_heads, n_kv, alibi, name):
    B, T, wq = q.shape
    Lc = kc.shape[1]
    grp = n_heads // n_kv
    v_off = n_kv * HEAD_DIM

    def kernel(q_ref, kc_ref, vc_ref, kvn_ref, *rest):
        if alibi:
            e_ref, o_ref = rest
            ii = lax.broadcasted_iota(jnp.int32, (T, Lc), 0)
            jj = lax.broadcasted_iota(jnp.int32, (T, Lc), 1)
            dist_c = jnp.abs(ii + Lc - jj).astype(F32)
            ii = lax.broadcasted_iota(jnp.int32, (T, T), 0)
            jj = lax.broadcasted_iota(jnp.int32, (T, T), 1)
            dist_n = jnp.abs(ii - jj).astype(F32)
        else:
            bc_ref, bn_ref, o_ref = rest
        q_all = q_ref[0] * (HEAD_DIM ** -0.5)
        for g in range(n_kv):
            sl = slice(g * HEAD_DIM, (g + 1) * HEAD_DIM)
            slv = slice(v_off + g * HEAD_DIM, v_off + (g + 1) * HEAD_DIM)
            k_c = kc_ref[0, :, sl].astype(BF16)
            v_c = vc_ref[0, :, sl].astype(BF16)
            k_n = kvn_ref[0, :, sl].astype(BF16)
            v_n = kvn_ref[0, :, slv].astype(BF16)
            for j in range(grp):
                hq = g * grp + j
                qh = q_all[:, hq * HEAD_DIM:(hq + 1) * HEAD_DIM]
                if alibi:
                    slope = -_alibi_slope(hq)
                    b_c, b_n, sink = dist_c * slope, dist_n * slope, e_ref[hq]
                else:
                    b_c, b_n, sink = bc_ref[hq], bn_ref[hq], None
                o = _attend(qh, [(k_c, v_c, b_c, None), (k_n, v_n, b_n, None)], sink)
                o_ref[0, :, hq * HEAD_DIM:(hq + 1) * HEAD_DIM] = o.astype(BF16)

    in_specs = [
        pl.BlockSpec((1, T, wq), lambda b: (b, 0, 0)),
        pl.BlockSpec((1, Lc, kc.shape[2]), lambda b: (b, 0, 0)),
        pl.BlockSpec((1, Lc, vc.shape[2]), lambda b: (b, 0, 0)),
        pl.BlockSpec((1, T, kvn.shape[2]), lambda b: (b, 0, 0)),
    ]
    if alibi:
        in_specs.append(pl.BlockSpec(memory_space=pltpu.SMEM))
        args = (q, kc, vc, kvn, extra)
    else:
        in_specs += [_const_spec(extra[0].shape), _const_spec(extra[1].shape)]
        args = (q, kc, vc, kvn, extra[0], extra[1])
    return pl.pallas_call(
        kernel,
        grid=(B,),
        in_specs=in_specs,
        out_specs=pl.BlockSpec((1, T, wq), lambda b: (b, 0, 0)),
        out_shape=jax.ShapeDtypeStruct((B, T, wq), BF16),
        compiler_params=_params(("arbitrary",)),
        name=name,
    )(*args)


def _relbias_kernel(tab_ref, o_ref):
    nt = tab_ref.shape[1]
    hi, mid, lo = _split3(tab_ref[...])
    tt = lax.broadcasted_iota(jnp.int32, (nt, BAND_C), 0)
    jj = lax.broadcasted_iota(jnp.int32, (nt, BAND_C), 1)

    def body(i, carry):
        idx = jnp.clip(i - jj + NPREV_C * CHUNK, -MAX_REL_C, MAX_REL_C) + MAX_REL_C
        oh = jnp.where(tt == idx, 1.0, 0.0).astype(BF16)
        o_ref[i] = _dot(hi, oh) + _dot(mid, oh) + _dot(lo, oh)
        return carry

    lax.fori_loop(0, CHUNK, body, 0)


def _relbias_call(table):
    nh, nt = table.shape
    ntp = ((nt + 127) // 128) * 128
    tab = jnp.pad(table, ((0, 0), (0, ntp - nt)))
    out = pl.pallas_call(
        _relbias_kernel,
        out_shape=jax.ShapeDtypeStruct((CHUNK, nh, BAND_C), F32),
        compiler_params=_params(None),
        name="relbias",
    )(tab)
    return jnp.transpose(out, (1, 0, 2))


def _mlstm_call(qkb, vb, og, gates, conv_w, conv_b, bg, hnorm, init, Lc, name):
    B, L, _ = qkb.shape
    nC = L // Lc
    zero_init = init is None
    k_scale = DH_B ** -0.5

    def kernel(*refs):
        if zero_init:
            (qkb_ref, vb_ref, og_ref, gt_ref, cw_ref, cb_ref, bg_ref, hn_ref,
             hb_ref, C_ref, n_ref, m_ref, tail_ref, ext_ref) = refs
        else:
            (qkb_ref, vb_ref, og_ref, gt_ref, cw_ref, cb_ref, bg_ref, hn_ref,
             C0_ref, n0_ref, m0_ref, conv0_ref,
             hb_ref, C_ref, n_ref, m_ref, tail_ref, ext_ref) = refs
        c = pl.program_id(0)

        @pl.when(c == 0)
        def _():
            if zero_init:
                C_ref[...] = jnp.zeros(C_ref.shape, F32)
                n_ref[...] = jnp.zeros(n_ref.shape, F32)
                m_ref[...] = jnp.zeros(m_ref.shape, F32)
                tail_ref[...] = jnp.zeros(tail_ref.shape, F32)
            else:
                C_ref[...] = C0_ref[...]
                n_ref[...] = n0_ref[...]
                m_ref[...] = m0_ref[...]
                tail_ref[...] = conv0_ref[...]

        r_i = lax.broadcasted_iota(jnp.int32, (Lc, Lc), 0)
        c_i = lax.broadcasted_iota(jnp.int32, (Lc, Lc), 1)
        tri = r_i >= c_i
        tri_bf = jnp.where(tri, 1.0, 0.0).astype(BF16)
        r_u = lax.broadcasted_iota(jnp.int32, (GATE_W, GATE_W), 0)
        c_u = lax.broadcasted_iota(jnp.int32, (GATE_W, GATE_W), 1)
        upper_bf = jnp.where(r_u <= c_u, 1.0, 0.0).astype(BF16)
        cw = cw_ref[...]
        cb = cb_ref[...]
        bgv = bg_ref[...]
        hn = hn_ref[...]

        def body(b, carry):
            u = qkb_ref[b].astype(F32)
            ext_ref[0:TAIL_ROWS, :] = tail_ref[b]
            ext_ref[TAIL_ROWS:TAIL_ROWS + Lc, :] = u
            y = cb + cw[CONV_W - 1:CONV_W] * u
            for j in range(CONV_W - 1):
                y = y + cw[j:j + 1] * ext_ref[pl.ds(TAIL_ROWS - (CONV_W - 1) + j, Lc), :]
            tail_ref[b] = ext_ref[Lc:Lc + TAIL_ROWS, :]
            qk = y * _sigmoid(y)

            gb = gt_ref[b] + bgv
            lf = _log_sigmoid(gb)
            l_hi, l_mid, l_lo = _split3(lf)
            b_col = _dot(tri_bf, l_hi) + _dot(tri_bf, l_mid) + _dot(tri_bf, l_lo)
            g_pad = jnp.concatenate([gb, jnp.zeros((GATE_W - Lc, GATE_W), F32)], axis=0)
            g_t = g_pad.T
            lf_t = _log_sigmoid(g_t[0:2 * H_B, :])
            t_hi, t_mid, t_lo = _split3(lf_t)
            b_row = _dot(t_hi, upper_bf) + _dot(t_mid, upper_bf) + _dot(t_lo, upper_bf)
            a_row_all = g_t[0:H_B, :] - b_row[H_B:2 * H_B, :]

            ogv = og_ref[b].astype(F32)
            vv = vb_ref[b]
            for h in range(H_B):
                hs = slice(h * DH_B, (h + 1) * DH_B)
                qh = qk[:, hs]
                kh = qk[:, W_B + h * DH_B:W_B + (h + 1) * DH_B] * k_scale
                vh = vv[:, hs]
                qb = qh.astype(BF16)
                kb = kh.astype(BF16)
                a_row = a_row_all[h:h + 1, 0:Lc]
                bcol = b_col[:, H_B + h:H_B + h + 1]
                a_col = gb[:, h:h + 1] - bcol
                m_prev = m_ref[b, h:h + 1, 0:1]
                amat = jnp.where(tri, a_row, -jnp.inf)
                mc = jnp.maximum(amat.max(axis=-1, keepdims=True), m_prev)
                dm = jnp.exp(amat - mc)
                w = _dot_nt(qb, kb) * dm
                c_h = C_ref[b, h]
                a_int = jnp.exp(m_prev - mc)
                num = a_int * _dot_nt(qb, c_h.astype(BF16)) + _dot(w.astype(BF16), vh)
                n_h = n_ref[b, h:h + 1, :]
                den = a_int * jnp.sum(qh * n_h, axis=-1, keepdims=True) + w.sum(axis=-1, keepdims=True)
                hh = num / jnp.maximum(jnp.abs(den), jnp.exp(-(bcol + mc)))
                hh = _rms(hh, hn[:, hs]) * _sigmoid(ogv[:, hs])
                hb_ref[b, :, hs] = hh.astype(BF16)

                m_end = mc[Lc - 1:Lc, :]
                a_end = jnp.exp(m_prev - m_end)
                w_end = jnp.exp(a_col - m_end)
                vw = vh.astype(F32) * w_end
                zpad = jnp.zeros((DH_B - Lc, DH_B), F32)
                vw_t = jnp.concatenate([vw, zpad], axis=0).T.astype(BF16)
                k_p = jnp.concatenate([kh, zpad], axis=0).astype(BF16)
                C_ref[b, h] = a_end * c_h + _dot(vw_t, k_p)
                n_ref[b, h:h + 1, :] = a_end * n_h + jnp.sum(kh * w_end, axis=0, keepdims=True)
                m_ref[b, h:h + 1, :] = jnp.broadcast_to(bcol[Lc - 1:Lc, :] + m_end, (1, GATE_W))
            return carry

        lax.fori_loop(0, B, body, 0)

    w2 = 2 * W_B
    in_specs = [
        pl.BlockSpec((B, Lc, w2), lambda c: (0, c, 0)),
        pl.BlockSpec((B, Lc, W_B), lambda c: (0, c, 0)),
        pl.BlockSpec((B, Lc, W_B), lambda c: (0, c, 0)),
        pl.BlockSpec((B, Lc, GATE_W), lambda c: (0, c, 0)),
        _const_spec((CONV_W, w2)),
        _const_spec((1, w2)),
        _const_spec((1, GATE_W)),
        _const_spec((1, W_B)),
    ]
    args = [qkb, vb, og, gates, conv_w, conv_b, bg, hnorm]
    if not zero_init:
        in_specs += [
            _const_spec((B, H_B, DH_B, DH_B)),
            _const_spec((B, H_B, DH_B)),
            _const_spec((B, H_B, GATE_W)),
            _const_spec((B, TAIL_ROWS, w2)),
        ]
        args += list(init)
    out_specs = [
        pl.BlockSpec((B, Lc, W_B), lambda c: (0, c, 0)),
        pl.BlockSpec((B, H_B, DH_B, DH_B), lambda c: (0, 0, 0, 0)),
        pl.BlockSpec((B, H_B, DH_B), lambda c: (0, 0, 0)),
        pl.BlockSpec((B, H_B, GATE_W), lambda c: (0, 0, 0)),
    ]
    out_shape = [
        jax.ShapeDtypeStruct((B, L, W_B), BF16),
        jax.ShapeDtypeStruct((B, H_B, DH_B, DH_B), F32),
        jax.ShapeDtypeStruct((B, H_B, DH_B), F32),
        jax.ShapeDtypeStruct((B, H_B, GATE_W), F32),
    ]
    return pl.pallas_call(
        kernel,
        grid=(nC,),
        in_specs=in_specs,
        out_specs=out_specs,
        out_shape=out_shape,
        scratch_shapes=[pltpu.VMEM((B, TAIL_ROWS, w2), F32), pltpu.VMEM((TAIL_ROWS + Lc, w2), F32)],
        compiler_params=_params(("arbitrary",)),
        name=name,
    )(*args)


def _outffn_call(parts, wo_parts, x, gt1, sc2, sh2, gt2, ng, wgu, wd, tm, name):
    B, L, d = x.shape
    nT = L // tm
    n_in = len(parts)
    n_ff = wgu.shape[0]
    fc = wd.shape[1]

    def kernel(*refs):
        a_refs = refs[:n_in]
        wo_refs = refs[n_in:2 * n_in]
        (x_ref, gt1_ref, sc2_ref, sh2_ref, gt2_ref, ng_ref, wgu_ref, wd_ref,
         o_ref, h_ref, acc_ref) = refs[2 * n_in:]
        y = _dot(a_refs[0][0], wo_refs[0][...])
        for a_ref, wo_ref in zip(a_refs[1:], wo_refs[1:]):
            y = y + _dot(a_ref[0], wo_ref[...])
        x1 = x_ref[0] + gt1_ref[0] * _rms(y, ng_ref[1:2, :])
        o_ref[0] = x1
        h_ref[...] = (_rms(x1, ng_ref[2:3, :]) * (1.0 + sc2_ref[0]) + sh2_ref[0]).astype(BF16)
        acc_ref[...] = jnp.zeros(acc_ref.shape, F32)

        def body(j, carry):
            gu = _dot(h_ref[...], wgu_ref[j])
            gpart = gu[:, :fc]
            z = (gpart * _sigmoid(gpart) * gu[:, fc:]).astype(BF16)
            acc_ref[...] += _dot(z, wd_ref[j])
            return carry

        lax.fori_loop(0, n_ff, body, 0)
        o_ref[0] = o_ref[0] + gt2_ref[0] * _rms(acc_ref[...], ng_ref[3:4, :])

    mod_spec = pl.BlockSpec((1, 1, d), lambda b, t: (b, 0, 0))
    in_specs = [pl.BlockSpec((1, tm, p.shape[2]), lambda b, t: (b, t, 0)) for p in parts]
    in_specs += [_const_spec(w.shape) for w in wo_parts]
    in_specs += [
        pl.BlockSpec((1, tm, d), lambda b, t: (b, t, 0)),
        mod_spec, mod_spec, mod_spec, mod_spec,
        _const_spec(ng.shape),
        _const_spec(wgu.shape),
        _const_spec(wd.shape),
    ]
    return pl.pallas_call(
        kernel,
        grid=(B, nT),
        in_specs=in_specs,
        out_specs=pl.BlockSpec((1, tm, d), lambda b, t: (b, t, 0)),
        out_shape=jax.ShapeDtypeStruct((B, L, d), F32),
        scratch_shapes=[pltpu.VMEM((tm, d), BF16), pltpu.VMEM((tm, d), F32)],
        compiler_params=_params(("arbitrary", "arbitrary")),
        name=name,
    )(*parts, *wo_parts, x, gt1, sc2, sh2, gt2, ng, wgu, wd)


def _prep_w_in_ab(w):
    o_q, o_k, o_v = 0, W_A, W_A + KV_A
    o_qk = W_A + 2 * KV_A
    o_vb = o_qk + 2 * W_B
    o_g = o_vb + W_B
    o_og = o_g + 2 * H_B
    d = w.shape[0]
    cols = [w[:, o_q:o_qk], w[:, o_qk:o_vb], w[:, o_vb:o_g], w[:, o_og:o_og + W_B],
            w[:, o_g:o_og], jnp.zeros((d, GATE_W - 2 * H_B), w.dtype)]
    return jnp.concatenate(cols, axis=1).astype(BF16)


def _prep_ffn(wg, wu, wd):
    d, dff = wg.shape
    n = dff // FF_CHUNK
    g3 = wg.reshape(d, n, FF_CHUNK)
    u3 = wu.reshape(d, n, FF_CHUNK)
    wgu = jnp.transpose(jnp.concatenate([g3, u3], axis=2), (1, 0, 2)).astype(BF16)
    return wgu, wd.reshape(n, FF_CHUNK, d).astype(BF16)


def kernel(x_prompt, x_sample, cache_a_k, cache_a_v, state_b_c, state_b_n, state_b_m, state_b_conv, cache_c_k, cache_c_v, c_prompt, c_sample, w_in_ab, sink_a, conv_w_b, conv_b_b, b_gates_b, hnorm_b, w_out_ab, w_in_c, relbias_c, w_out_c, w_ada, b_ada, norm_g, w_ffn_gate, w_ffn_up, w_ffn_down):
    Bp, Lp, d = x_prompt.shape
    Bs, Ls, _ = x_sample.shape
    depth = w_ada.shape[0]
    tm_p = min(ROW_TILE, Lp)
    assert Lp % tm_p == 0 and tm_p == ROW_TILE and Lp % CHUNK == 0
    assert (Bp + Bs) % 8 == 0 and Ls % 8 == 0 and Ls >= CONV_W - 1

    mods = _ada_call(jnp.concatenate([c_prompt, c_sample], axis=0), w_ada, b_ada)

    def mod_parts(i, lo, hi):
        m = mods[i, lo:hi].reshape(hi - lo, 1, 6, d)
        return [m[:, :, k, :] for k in range(6)]

    c_qa, c_kv, c_qkb = 0, W_A, W_A + 2 * KV_A
    c_vb = c_qkb + 2 * W_B
    c_og = c_vb + W_B
    c_gt = c_og + W_B

    xp, xs = x_prompt, x_sample
    st_p = {k: [] for k in ("a_k", "a_v", "b_c", "b_n", "b_m", "b_conv", "c_k", "c_v")}
    st_s = {k: [] for k in st_p}

    for i in range(depth):
        j = i // 2
        ng = norm_g[i]
        g0 = ng[0:1]
        shp1, scp1, gtp1, shp2, scp2, gtp2 = mod_parts(i, 0, Bp)
        shs1, scs1, gts1, shs2, scs2, gts2 = mod_parts(i, Bp, Bp + Bs)
        wgu, wdn = _prep_ffn(w_ffn_gate[i], w_ffn_up[i], w_ffn_down[i])

        if i % 2 == 0:
            w_in = _prep_w_in_ab(w_in_ab[j])
            wo = w_out_ab[j].astype(BF16)
            wo_parts = [wo[:W_A], wo[W_A:]]
            conv_w = conv_w_b[j]
            conv_b = conv_b_b[j].reshape(1, 2 * W_B)
            bg = jnp.pad(b_gates_b[j], (0, GATE_W - 2 * H_B)).reshape(1, GATE_W)
            hn = hnorm_b[j].reshape(1, W_B)
            sinks = sink_a[j]

            segs_p = [
                (c_qa, W_A, BF16, False, 0),
                (c_kv, 2 * KV_A, BF16, True, WINDOW_A),
                (c_qkb, 2 * W_B, BF16, False, TAIL_ROWS),
                (c_vb, W_B, BF16, False, 0),
                (c_og, W_B, BF16, False, 0),
                (c_gt, GATE_W, F32, False, 0),
            ]
            qa, kva, qkb, vb, og, gts, kv_tail, conv_tail = _inproj_call(
                xp, scp1, shp1, g0, w_in, segs_p, tm_p, 1, "inproj_ab_p")
            out_a = _band_attn_call(qa, kva, sinks, HQ_A, HKV_A, BAND_A, True, "attn_a_p")
            hb, s_c, s_n, s_m = _mlstm_call(qkb, vb, og, gts, conv_w, conv_b, bg, hn, None, CHUNK, "mlstm_p")
            xp = _outffn_call([out_a, hb], wo_parts, xp, gtp1, scp2, shp2, gtp2, ng, wgu, wdn, tm_p, "outffn_ab_p")
            st_p["a_k"].append(kv_tail[:, :, :KV_A].reshape(Bp, WINDOW_A, HKV_A, HEAD_DIM))
            st_p["a_v"].append(kv_tail[:, :, KV_A:].reshape(Bp, WINDOW_A, HKV_A, HEAD_DIM))
            st_p["b_c"].append(s_c)
            st_p["b_n"].append(s_n)
            st_p["b_m"].append(s_m[:, :, 0])
            st_p["b_conv"].append(conv_tail[:, TAIL_ROWS - (CONV_W - 1):, :])

            segs_s = [
                (c_qa, W_A, BF16, False, 0),
                (c_kv, 2 * KV_A, F32, False, 0),
                (c_qkb, 2 * W_B, BF16, False, TAIL_ROWS),
                (c_vb, W_B, BF16, False, 0),
                (c_og, W_B, BF16, False, 0),
                (c_gt, GATE_W, F32, False, 0),
            ]
            qa, kvn, qkb, vb, og, gts, conv_tail = _inproj_call(
                xs, scs1, shs1, g0, w_in, segs_s, Ls, 0, "inproj_ab_s")
            la = cache_a_k.shape[2]
            out_a = _cache_attn_call(qa, cache_a_k[j].reshape(Bs, la, KV_A), cache_a_v[j].reshape(Bs, la, KV_A),
                                     kvn, sinks, HQ_A, HKV_A, True, "attn_a_s")
            init = (state_b_c[j], state_b_n[j],
                    jnp.broadcast_to(state_b_m[j][:, :, None], (Bs, H_B, GATE_W)),
                    jnp.pad(state_b_conv[j], ((0, 0), (TAIL_ROWS - (CONV_W - 1), 0), (0, 0))))
            hb, s_c, s_n, s_m = _mlstm_call(qkb, vb, og, gts, conv_w, conv_b, bg, hn, init, Ls, "mlstm_s")
            xs = _outffn_call([out_a, hb], wo_parts, xs, gts1, scs2, shs2, gts2, ng, wgu, wdn, Ls, "outffn_ab_s")
            st_s["a_k"].append(kvn[:, :, :KV_A].reshape(Bs, Ls, HKV_A, HEAD_DIM))
            st_s["a_v"].append(kvn[:, :, KV_A:].reshape(Bs, Ls, HKV_A, HEAD_DIM))
            st_s["b_c"].append(s_c)
            st_s["b_n"].append(s_n)
            st_s["b_m"].append(s_m[:, :, 0])
            st_s["b_conv"].append(conv_tail[:, TAIL_ROWS - (CONV_W - 1):, :])
        else:
            w_in = w_in_c[j].astype(BF16)
            wo_parts = [w_out_c[j].astype(BF16)]
            bias = _relbias_call(relbias_c[j])
            lc = cache_c_k.shape[2]
            tail_c = NPREV_C * CHUNK

            segs_p = [
                (0, W_C, BF16, False, 0),
                (W_C, 2 * W_C, BF16, True, tail_c),
            ]
            qc, kvc, kv_tail = _inproj_call(xp, scp1, shp1, g0, w_in, segs_p, tm_p, 1, "inproj_c_p")
            out_c = _band_attn_call(qc, kvc, bias, H_C, H_C, BAND_C, False, "attn_c_p")
            xp = _outffn_call([out_c], wo_parts, xp, gtp1, scp2, shp2, gtp2, ng, wgu, wdn, tm_p, "outffn_c_p")
            st_p["c_k"].append(kv_tail[:, :, :W_C].reshape(Bp, tail_c, H_C, HEAD_DIM))
            st_p["c_v"].append(kv_tail[:, :, W_C:].reshape(Bp, tail_c, H_C, HEAD_DIM))

            segs_s = [
                (0, W_C, BF16, False, 0),
                (W_C, 2 * W_C, F32, False, 0),
            ]
            qc, kvn = _inproj_call(xs, scs1, shs1, g0, w_in, segs_s, Ls, 0, "inproj_c_s")
            off = BAND_C - CHUNK - lc
            bias_c = bias[:, :Ls, off:off + lc]
            bias_n = bias[:, :Ls, off + lc:off + lc + Ls]
            out_c = _cache_attn_call(qc, cache_c_k[j].reshape(Bs, lc, W_C), cache_c_v[j].reshape(Bs, lc, W_C),
                                     kvn, (bias_c, bias_n), H_C, H_C, False, "attn_c_s")
            xs = _outffn_call([out_c], wo_parts, xs, gts1, scs2, shs2, gts2, ng, wgu, wdn, Ls, "outffn_c_s")
            st_s["c_k"].append(kvn[:, :, :W_C].reshape(Bs, Ls, H_C, HEAD_DIM))
            st_s["c_v"].append(kvn[:, :, W_C:].reshape(Bs, Ls, H_C, HEAD_DIM))

    order = ("a_k", "a_v", "b_c", "b_n", "b_m", "b_conv", "c_k", "c_v")
    outs = [xp, xs]
    outs += [jnp.stack(st_p[k]) for k in order]
    outs += [jnp.stack(st_s[k]) for k in order]
    return tuple(outs)
```

```python
import jax
import jax.numpy as jnp
from jax import lax
from jax.experimental import pallas as pl
from jax.experimental.pallas import tpu as pltpu

F32 = jnp.float32
BF16 = jnp.bfloat16

D_MODEL = 1024
CHUNK = 64
HEAD_DIM = 64
HQ_A = 8
HKV_A = 2
WINDOW_A = 128
H_B = 4
DH_B = 128
CONV_W = 4
H_C = 16
NPREV_C = 8
MAX_REL_C = 256
EPS = 1e-6
NEG = -1e30
W_A = HQ_A * HEAD_DIM
KV_A = HKV_A * HEAD_DIM
W_B = H_B * DH_B
W_C = H_C * HEAD_DIM
BAND_A = WINDOW_A + CHUNK
BAND_C = (NPREV_C + 1) * CHUNK
FF_CHUNK = 256
GATE_W = 128
TAIL_ROWS = 8
ROW_TILE = 512
V7X_VMEM_LIMIT = 56 * 1024 * 1024
QK_SCALE = HEAD_DIM ** -0.5
PERM_A = tuple(h for pair in zip(range(HQ_A // HKV_A), range(HQ_A // HKV_A, HQ_A)) for h in pair)
CHUNKS_PER_STEP_A = 4
CHUNKS_PER_STEP_C = 2


def _params(sem, vmem=V7X_VMEM_LIMIT, flags=None):
    return pltpu.CompilerParams(dimension_semantics=sem, vmem_limit_bytes=vmem, flags=flags)


def _const_spec(shape):
    nd = len(shape)
    return pl.BlockSpec(shape, lambda *_: (0,) * nd, pipeline_mode=pl.Buffered(1))


def _split3(x):
    hi = x.astype(BF16)
    r1 = x - hi.astype(F32)
    mid = r1.astype(BF16)
    lo = (r1 - mid.astype(F32)).astype(BF16)
    return hi, mid, lo


def _dot(a, b):
    return jnp.dot(a, b, preferred_element_type=F32)


def _dot_nt(a, b):
    return lax.dot_general(a, b, (((1,), (1,)), ((), ())), preferred_element_type=F32)


def _rms(x, g):
    return x * lax.rsqrt(jnp.mean(x * x, axis=-1, keepdims=True) + EPS) * g


def _sigmoid(x):
    return 1.0 / (1.0 + jnp.exp(-x))


def _log_sigmoid(x):
    return jnp.minimum(x, 0.0) - jnp.log(1.0 + jnp.exp(-jnp.abs(x)))


def _ada_kernel(c_ref, w_ref, b_ref, o_ref):
    c = c_ref[...]
    s = c * _sigmoid(c)
    s_hi, s_mid, _ = _split3(s)
    w_hi, w_mid, _ = _split3(w_ref[0])
    acc = _dot(s_hi, w_hi) + _dot(s_hi, w_mid) + _dot(s_mid, w_hi)
    o_ref[0] = acc + b_ref[0]


def _ada_call(c_all, w_ada, b_ada):
    depth, d, n6 = w_ada.shape
    nb = c_all.shape[0]
    tn = 512
    return pl.pallas_call(
        _ada_kernel,
        grid=(depth, n6 // tn),
        in_specs=[
            pl.BlockSpec((nb, d), lambda i, j: (0, 0)),
            pl.BlockSpec((1, d, tn), lambda i, j: (i, 0, j)),
            pl.BlockSpec((1, 1, tn), lambda i, j: (i, 0, j)),
        ],
        out_specs=pl.BlockSpec((1, nb, tn), lambda i, j: (i, 0, j)),
        out_shape=jax.ShapeDtypeStruct((depth, nb, n6), F32),
        compiler_params=_params(("arbitrary", "arbitrary")),
        name="ada_mod",
    )(c_all, w_ada, b_ada.reshape(depth, 1, n6))


def _inproj_call(x, sc, sh, g, w, segs, tm, pad, name):
    B, L, d = x.shape
    nT = L // tm
    P = w.shape[1]
    tails = [s for s in segs if s[4]]
    n_seg = len(segs)

    def kernel(x_ref, sc_ref, sh_ref, g_ref, w_ref, *outs):
        seg_refs = outs[:n_seg]
        tail_refs = outs[n_seg:]
        t = pl.program_id(1)

        def compute():
            h = (_rms(x_ref[0], g_ref[...]) * (1.0 + sc_ref[0]) + sh_ref[0]).astype(BF16)
            ti = 0
            for (c0, wd, dt, _, tail_rows), r in zip(segs, seg_refs):
                y = _dot(h, w_ref[:, c0:c0 + wd])
                r[0] = y.astype(dt)
                if tail_rows:
                    tr = tail_refs[ti]
                    ti += 1

                    @pl.when(t == nT - 1 + pad)
                    def _():
                        tr[0] = y[tm - tail_rows:, :]

        if pad:
            @pl.when(t == 0)
            def _():
                for (_, wd, dt, padded, _), r in zip(segs, seg_refs):
                    if padded:
                        r[0] = jnp.zeros((tm, wd), dt)

            pl.when(t > 0)(compute)
        else:
            compute()

    def row_idx(t):
        return jnp.maximum(t - 1, 0) if pad else t

    in_specs = [
        pl.BlockSpec((1, tm, d), lambda b, t: (b, row_idx(t), 0)),
        pl.BlockSpec((1, 1, d), lambda b, t: (b, 0, 0)),
        pl.BlockSpec((1, 1, d), lambda b, t: (b, 0, 0)),
        _const_spec((1, d)),
        _const_spec((d, P)),
    ]
    out_specs, out_shapes = [], []
    for (_, wd, dt, padded, _) in segs:
        if padded:
            out_specs.append(pl.BlockSpec((1, tm, wd), lambda b, t: (b, t, 0)))
            out_shapes.append(jax.ShapeDtypeStruct((B, tm + L, wd), dt))
        else:
            out_specs.append(pl.BlockSpec((1, tm, wd), lambda b, t: (b, row_idx(t), 0)))
            out_shapes.append(jax.ShapeDtypeStruct((B, L, wd), dt))
    for (_, wd, _, _, tail_rows) in tails:
        out_specs.append(pl.BlockSpec((1, tail_rows, wd), lambda b, t: (b, 0, 0)))
        out_shapes.append(jax.ShapeDtypeStruct((B, tail_rows, wd), F32))

    return pl.pallas_call(
        kernel,
        grid=(B, nT + pad),
        in_specs=in_specs,
        out_specs=out_specs,
        out_shape=out_shapes,
        compiler_params=_params(("arbitrary", "arbitrary")),
        name=name,
    )(x, sc, sh, g, w)


def _attend_grouped(q_tiles, pieces, sink_col, lq):
    wg = q_tiles[0].shape[1]
    nb = wg // HEAD_DIM
    blk = lax.shift_right_logical(lax.broadcasted_iota(jnp.int32, (lq, wg), 1), HEAD_DIM.bit_length() - 1)
    rows = []
    for qt in q_tiles:
        for r in range(nb):
            rows.append(jnp.where(blk == r, qt, 0.0).astype(BF16))
    qbd = jnp.concatenate(rows, axis=0)
    ss = []
    for k, _, bias, valid in pieces:
        s = _dot_nt(qbd, k) + bias
        if valid is not None:
            s = jnp.where(valid, s, NEG)
        ss.append(s)
    m = ss[0].max(axis=-1, keepdims=True)
    for s in ss[1:]:
        m = jnp.maximum(m, s.max(axis=-1, keepdims=True))
    if sink_col is not None:
        m = jnp.maximum(m, sink_col)
    den = None
    o = None
    for s, (_, v, _, _) in zip(ss, pieces):
        p = jnp.exp(s - m)
        ps = p.sum(axis=-1, keepdims=True)
        pv = _dot(p.astype(BF16), v)
        den = ps if den is None else den + ps
        o = pv if o is None else o + pv
    if sink_col is not None:
        den = den + jnp.exp(sink_col - m)
    o = o / den
    outs = []
    for t in range(len(q_tiles)):
        base = t * nb * lq
        acc = o[base + (nb - 1) * lq:base + nb * lq]
        for r in range(nb - 2, -1, -1):
            acc = jnp.where(blk == r, o[base + r * lq:base + (r + 1) * lq], acc)
        outs.append(acc)
    return outs


def _alibi_rows(sinks, lq):
    perm = jnp.array(PERM_A)
    slopes = jnp.exp2(-8.0 * jnp.arange(1, HQ_A + 1, dtype=F32) / HQ_A)
    rc = jnp.stack([-slopes[perm], sinks.astype(F32)[perm]], axis=1)
    return jnp.repeat(rc, lq, axis=0)


def _band_attn_call(q, kv, extra, wg, q_groups, band, alibi, nq, name):
    B, L, wq = q.shape
    nsteps = L // (CHUNK * nq)
    hist = band - CHUNK
    kvw = kv.shape[2]
    v_off = kvw // 2
    nb = wg // HEAD_DIM

    def kernel(q_ref, kv_ref, e_ref, o_ref):
        step = pl.program_id(1)
        jj = lax.broadcasted_iota(jnp.int32, (1, band), 1)
        if alibi:
            m_rows = e_ref.shape[0]
            ii = lax.broadcasted_iota(jnp.int32, (m_rows, band), 0) & (CHUNK - 1)
            dist = jnp.abs(ii + hist - lax.broadcasted_iota(jnp.int32, (m_rows, band), 1)).astype(F32)
            bias_a = e_ref[:, 0:1] * dist
            sink = e_ref[:, 1:2]
        for cc in range(nq):
            c = step * nq + cc
            start = pl.multiple_of(c * CHUNK + ROW_TILE - hist, CHUNK)
            valid = (c * CHUNK - hist + jj) >= 0
            rs = slice(cc * CHUNK, (cc + 1) * CHUNK)
            for kv0, q_offs in q_groups:
                kb = kv_ref[0, pl.ds(start, band), kv0:kv0 + wg]
                vb = kv_ref[0, pl.ds(start, band), v_off + kv0:v_off + kv0 + wg]
                tiles = [q_ref[0, rs, o:o + wg].astype(F32) for o in q_offs]
                if alibi:
                    outs = _attend_grouped(tiles, [(kb, vb, bias_a, valid)], sink, CHUNK)
                else:
                    r0 = (q_offs[0] // HEAD_DIM) * CHUNK
                    bias = e_ref[r0:r0 + len(q_offs) * nb * CHUNK, :]
                    outs = _attend_grouped(tiles, [(kb, vb, bias, valid)], None, CHUNK)
                for o, res in zip(q_offs, outs):
                    o_ref[0, rs, o:o + wg] = res.astype(BF16)

    return pl.pallas_call(
        kernel,
        grid=(B, nsteps),
        in_specs=[
            pl.BlockSpec((1, nq * CHUNK, wq), lambda b, c: (b, c, 0)),
            pl.BlockSpec((1, ROW_TILE + L, kvw), lambda b, c: (b, 0, 0)),
            _const_spec(extra.shape),
        ],
        out_specs=pl.BlockSpec((1, nq * CHUNK, wq), lambda b, c: (b, c, 0)),
        out_shape=jax.ShapeDtypeStruct((B, L, wq), BF16),
        compiler_params=_params(("arbitrary", "arbitrary")),
        name=name,
    )(q, kv, extra)


def _cache_attn_call(q, kc, vc, kvn, extra, wg, q_groups, alibi, name):
    B, T, wq = q.shape
    Lc = kc.shape[1]
    v_off = kvn.shape[2] // 2
    nb = wg // HEAD_DIM
    assert T & (T - 1) == 0

    def kernel(q_ref, kc_ref, vc_ref, kvn_ref, *rest):
        if alibi:
            e_ref, o_ref = rest
            m_rows = e_ref.shape[0]
            ii = lax.broadcasted_iota(jnp.int32, (m_rows, Lc), 0) & (T - 1)
            dist_c = jnp.abs(ii + Lc - lax.broadcasted_iota(jnp.int32, (m_rows, Lc), 1)).astype(F32)
            ii = lax.broadcasted_iota(jnp.int32, (m_rows, T), 0) & (T - 1)
            dist_n = jnp.abs(ii - lax.broadcasted_iota(jnp.int32, (m_rows, T), 1)).astype(F32)
            slope = e_ref[:, 0:1]
            sink = e_ref[:, 1:2]
        else:
            bc_ref, bn_ref, o_ref = rest
        for kv0, q_offs in q_groups:
            k_c = kc_ref[0, :, kv0:kv0 + wg].astype(BF16)
            v_c = vc_ref[0, :, kv0:kv0 + wg].astype(BF16)
            k_n = kvn_ref[0, :, kv0:kv0 + wg].astype(BF16)
            v_n = kvn_ref[0, :, v_off + kv0:v_off + kv0 + wg].astype(BF16)
            tiles = [q_ref[0, :, o:o + wg].astype(F32) for o in q_offs]
            if alibi:
                outs = _attend_grouped(tiles, [(k_c, v_c, slope * dist_c, None), (k_n, v_n, slope * dist_n, None)],
                                       sink, T)
            else:
                r0 = (q_offs[0] // HEAD_DIM) * T
                m_rows = len(q_offs) * nb * T
                outs = _attend_grouped(tiles, [(k_c, v_c, bc_ref[r0:r0 + m_rows, :], None),
                                               (k_n, v_n, bn_ref[r0:r0 + m_rows, :], None)], None, T)
            for o, res in zip(q_offs, outs):
                o_ref[0, :, o:o + wg] = res.astype(BF16)

    in_specs = [
        pl.BlockSpec((1, T, wq), lambda b: (b, 0, 0)),
        pl.BlockSpec((1, Lc, kc.shape[2]), lambda b: (b, 0, 0)),
        pl.BlockSpec((1, Lc, vc.shape[2]), lambda b: (b, 0, 0)),
        pl.BlockSpec((1, T, kvn.shape[2]), lambda b: (b, 0, 0)),
    ]
    if alibi:
        in_specs.append(_const_spec(extra.shape))
        args = (q, kc, vc, kvn, extra)
    else:
        in_specs += [_const_spec(extra[0].shape), _const_spec(extra[1].shape)]
        args = (q, kc, vc, kvn, extra[0], extra[1])
    return pl.pallas_call(
        kernel,
        grid=(B,),
        in_specs=in_specs,
        out_specs=pl.BlockSpec((1, T, wq), lambda b: (b, 0, 0)),
        out_shape=jax.ShapeDtypeStruct((B, T, wq), BF16),
        compiler_params=_params(("arbitrary",)),
        name=name,
    )(*args)


def _relbias_kernel(tab_ref, o_ref):
    nt = tab_ref.shape[1]
    hi, mid, lo = _split3(tab_ref[...])
    tt = lax.broadcasted_iota(jnp.int32, (nt, BAND_C), 0)
    jj = lax.broadcasted_iota(jnp.int32, (nt, BAND_C), 1)

    def body(i, carry):
        idx = jnp.clip(i - jj + NPREV_C * CHUNK, -MAX_REL_C, MAX_REL_C) + MAX_REL_C
        oh = jnp.where(tt == idx, 1.0, 0.0).astype(BF16)
        o_ref[i] = _dot(hi, oh) + _dot(mid, oh) + _dot(lo, oh)
        return carry

    lax.fori_loop(0, CHUNK, body, 0)


def _relbias_call(table):
    nh, nt = table.shape
    ntp = ((nt + 127) // 128) * 128
    tab = jnp.pad(table, ((0, 0), (0, ntp - nt)))
    out = pl.pallas_call(
        _relbias_kernel,
        out_shape=jax.ShapeDtypeStruct((CHUNK, nh, BAND_C), F32),
        compiler_params=_params(None),
        name="relbias",
    )(tab)
    return jnp.transpose(out, (1, 0, 2))


def _mlstm_call(qkb, vb, og, gates, conv_w, conv_b, bg, hnorm, init, Lc, name):
    B, L, _ = qkb.shape
    nC = L // Lc
    zero_init = init is None
    k_scale = DH_B ** -0.5

    def kernel(*refs):
        if zero_init:
            (qkb_ref, vb_ref, og_ref, gt_ref, cw_ref, cb_ref, bg_ref, hn_ref,
             hb_ref, C_ref, n_ref, m_ref, tail_ref, ext_ref) = refs
        else:
            (qkb_ref, vb_ref, og_ref, gt_ref, cw_ref, cb_ref, bg_ref, hn_ref,
             C0_ref, n0_ref, m0_ref, conv0_ref,
             hb_ref, C_ref, n_ref, m_ref, tail_ref, ext_ref) = refs
        c = pl.program_id(0)

        @pl.when(c == 0)
        def _():
            if zero_init:
                C_ref[...] = jnp.zeros(C_ref.shape, F32)
                n_ref[...] = jnp.zeros(n_ref.shape, F32)
                m_ref[...] = jnp.zeros(m_ref.shape, F32)
                tail_ref[...] = jnp.zeros(tail_ref.shape, F32)
            else:
                C_ref[...] = C0_ref[...]
                n_ref[...] = n0_ref[...]
                m_ref[...] = m0_ref[...]
                tail_ref[...] = conv0_ref[...]

        r_i = lax.broadcasted_iota(jnp.int32, (Lc, Lc), 0)
        c_i = lax.broadcasted_iota(jnp.int32, (Lc, Lc), 1)
        tri = r_i >= c_i
        tri_bf = jnp.where(tri, 1.0, 0.0).astype(BF16)
        r_u = lax.broadcasted_iota(jnp.int32, (GATE_W, GATE_W), 0)
        c_u = lax.broadcasted_iota(jnp.int32, (GATE_W, GATE_W), 1)
        upper_bf = jnp.where(r_u <= c_u, 1.0, 0.0).astype(BF16)
        cw = cw_ref[...]
        cb = cb_ref[...]
        bgv = bg_ref[...]
        hn = hn_ref[...]

        def body(b, carry):
            u = qkb_ref[b].astype(F32)
            ext_ref[b, 0:TAIL_ROWS, :] = tail_ref[b]
            ext_ref[b, TAIL_ROWS:TAIL_ROWS + Lc, :] = u
            y = cb + cw[CONV_W - 1:CONV_W] * u
            for j in range(CONV_W - 1):
                y = y + cw[j:j + 1] * ext_ref[b, pl.ds(TAIL_ROWS - (CONV_W - 1) + j, Lc), :]
            tail_ref[b] = ext_ref[b, Lc:Lc + TAIL_ROWS, :]
            qk = y * _sigmoid(y)

            gb = gt_ref[b] + bgv
            lf = _log_sigmoid(gb)
            l_hi, l_mid, l_lo = _split3(lf)
            b_col = _dot(tri_bf, l_hi) + _dot(tri_bf, l_mid) + _dot(tri_bf, l_lo)
            g_pad = jnp.concatenate([gb, jnp.zeros((GATE_W - Lc, GATE_W), F32)], axis=0)
            g_t = g_pad.T
            lf_t = _log_sigmoid(g_t[0:2 * H_B, :])
            t_hi, t_mid, t_lo = _split3(lf_t)
            b_row = _dot(t_hi, upper_bf) + _dot(t_mid, upper_bf) + _dot(t_lo, upper_bf)
            a_row_all = g_t[0:H_B, :] - b_row[H_B:2 * H_B, :]

            ogv = og_ref[b].astype(F32)
            vv = vb_ref[b]
            for h in range(H_B):
                hs = slice(h * DH_B, (h + 1) * DH_B)
                qh = qk[:, hs]
                kh = qk[:, W_B + h * DH_B:W_B + (h + 1) * DH_B] * k_scale
                vh = vv[:, hs]
                qb = qh.astype(BF16)
                kb = kh.astype(BF16)
                a_row = a_row_all[h:h + 1, 0:Lc]
                bcol = b_col[:, H_B + h:H_B + h + 1]
                a_col = gb[:, h:h + 1] - bcol
                m_prev = m_ref[b, h:h + 1, 0:1]
                amat = jnp.where(tri, a_row, -jnp.inf)
                mc = jnp.maximum(amat.max(axis=-1, keepdims=True), m_prev)
                dm = jnp.exp(amat - mc)
                w = _dot_nt(qb, kb) * dm
                c_h = C_ref[b, h]
                a_int = jnp.exp(m_prev - mc)
                num = a_int * _dot_nt(qb, c_h.astype(BF16)) + _dot(w.astype(BF16), vh)
                n_h = n_ref[b, h:h + 1, :]
                den = a_int * jnp.sum(qh * n_h, axis=-1, keepdims=True) + w.sum(axis=-1, keepdims=True)
                hh = num / jnp.maximum(jnp.abs(den), jnp.exp(-(bcol + mc)))
                hh = _rms(hh, hn[:, hs]) * _sigmoid(ogv[:, hs])
                hb_ref[b, :, hs] = hh.astype(BF16)

                m_end = mc[Lc - 1:Lc, :]
                a_end = jnp.exp(m_prev - m_end)
                w_end = jnp.exp(a_col - m_end)
                vw = vh.astype(F32) * w_end
                zpad = jnp.zeros((DH_B - Lc, DH_B), F32)
                vw_t = jnp.concatenate([vw, zpad], axis=0).T.astype(BF16)
                k_p = jnp.concatenate([kh, zpad], axis=0).astype(BF16)
                C_ref[b, h] = a_end * c_h + _dot(vw_t, k_p)
                n_ref[b, h:h + 1, :] = a_end * n_h + jnp.sum(kh * w_end, axis=0, keepdims=True)
                m_ref[b, h:h + 1, :] = jnp.broadcast_to(bcol[Lc - 1:Lc, :] + m_end, (1, GATE_W))
            return carry

        lax.fori_loop(0, B, body, 0, unroll=2)

    w2 = 2 * W_B
    in_specs = [
        pl.BlockSpec((B, Lc, w2), lambda c: (0, c, 0)),
        pl.BlockSpec((B, Lc, W_B), lambda c: (0, c, 0)),
        pl.BlockSpec((B, Lc, W_B), lambda c: (0, c, 0)),
        pl.BlockSpec((B, Lc, GATE_W), lambda c: (0, c, 0)),
        _const_spec((CONV_W, w2)),
        _const_spec((1, w2)),
        _const_spec((1, GATE_W)),
        _const_spec((1, W_B)),
    ]
    args = [qkb, vb, og, gates, conv_w, conv_b, bg, hnorm]
    if not zero_init:
        in_specs += [
            _const_spec((B, H_B, DH_B, DH_B)),
            _const_spec((B, H_B, DH_B)),
            _const_spec((B, H_B, GATE_W)),
            _const_spec((B, TAIL_ROWS, w2)),
        ]
        args += list(init)
    out_specs = [
        pl.BlockSpec((B, Lc, W_B), lambda c: (0, c, 0)),
        pl.BlockSpec((B, H_B, DH_B, DH_B), lambda c: (0, 0, 0, 0)),
        pl.BlockSpec((B, H_B, DH_B), lambda c: (0, 0, 0)),
        pl.BlockSpec((B, H_B, GATE_W), lambda c: (0, 0, 0)),
    ]
    out_shape = [
        jax.ShapeDtypeStruct((B, L, W_B), BF16),
        jax.ShapeDtypeStruct((B, H_B, DH_B, DH_B), F32),
        jax.ShapeDtypeStruct((B, H_B, DH_B), F32),
        jax.ShapeDtypeStruct((B, H_B, GATE_W), F32),
    ]
    return pl.pallas_call(
        kernel,
        grid=(nC,),
        in_specs=in_specs,
        out_specs=out_specs,
        out_shape=out_shape,
        scratch_shapes=[pltpu.VMEM((B, TAIL_ROWS, w2), F32), pltpu.VMEM((B, TAIL_ROWS + Lc, w2), F32)],
        compiler_params=_params(("arbitrary",)),
        name=name,
    )(*args)


def _outffn_call(parts, wo_parts, x, gt1, sc2, sh2, gt2, ng, wgu, wd, tm, name):
    B, L, d = x.shape
    nT = L // tm
    n_in = len(parts)
    n_ff = wgu.shape[0]
    fc = wd.shape[1]

    def kernel(*refs):
        a_refs = refs[:n_in]
        wo_refs = refs[n_in:2 * n_in]
        (x_ref, gt1_ref, sc2_ref, sh2_ref, gt2_ref, ng_ref, wgu_ref, wd_ref,
         o_ref, h_ref, acc_ref) = refs[2 * n_in:]
        y = _dot(a_refs[0][0], wo_refs[0][...])
        for a_ref, wo_ref in zip(a_refs[1:], wo_refs[1:]):
            y = y + _dot(a_ref[0], wo_ref[...])
        x1 = x_ref[0] + gt1_ref[0] * _rms(y, ng_ref[1:2, :])
        o_ref[0] = x1
        h_ref[...] = (_rms(x1, ng_ref[2:3, :]) * (1.0 + sc2_ref[0]) + sh2_ref[0]).astype(BF16)
        acc_ref[...] = jnp.zeros(acc_ref.shape, F32)

        def body(j, carry):
            gu = _dot(h_ref[...], wgu_ref[j])
            gpart = gu[:, :fc]
            z = (gpart * _sigmoid(gpart) * gu[:, fc:]).astype(BF16)
            acc_ref[...] += _dot(z, wd_ref[j])
            return carry

        lax.fori_loop(0, n_ff, body, 0)
        o_ref[0] = o_ref[0] + gt2_ref[0] * _rms(acc_ref[...], ng_ref[3:4, :])

    mod_spec = pl.BlockSpec((1, 1, d), lambda b, t: (b, 0, 0))
    in_specs = [pl.BlockSpec((1, tm, p.shape[2]), lambda b, t: (b, t, 0)) for p in parts]
    in_specs += [_const_spec(w.shape) for w in wo_parts]
    in_specs += [
        pl.BlockSpec((1, tm, d), lambda b, t: (b, t, 0)),
        mod_spec, mod_spec, mod_spec, mod_spec,
        _const_spec(ng.shape),
        _const_spec(wgu.shape),
        _const_spec(wd.shape),
    ]
    return pl.pallas_call(
        kernel,
        grid=(B, nT),
        in_specs=in_specs,
        out_specs=pl.BlockSpec((1, tm, d), lambda b, t: (b, t, 0)),
        out_shape=jax.ShapeDtypeStruct((B, L, d), F32),
        scratch_shapes=[pltpu.VMEM((tm, d), BF16), pltpu.VMEM((tm, d), F32)],
        compiler_params=_params(("arbitrary", "arbitrary")),
        name=name,
    )(*parts, *wo_parts, x, gt1, sc2, sh2, gt2, ng, wgu, wd)


def _perm_heads(w, axis):
    blocks = [lax.slice_in_dim(w, h * HEAD_DIM, (h + 1) * HEAD_DIM, axis=axis) for h in PERM_A]
    return jnp.concatenate(blocks, axis=axis)


def _prep_w_in_ab(w):
    o_k = W_A
    o_qk = W_A + 2 * KV_A
    o_vb = o_qk + 2 * W_B
    o_g = o_vb + W_B
    o_og = o_g + 2 * H_B
    d = w.shape[0]
    cols = [_perm_heads(w[:, :W_A], 1) * QK_SCALE, w[:, o_k:o_qk], w[:, o_qk:o_vb], w[:, o_vb:o_g],
            w[:, o_og:o_og + W_B], w[:, o_g:o_og], jnp.zeros((d, GATE_W - 2 * H_B), w.dtype)]
    return jnp.concatenate(cols, axis=1).astype(BF16)


def _prep_ffn(wg, wu, wd):
    d, dff = wg.shape
    n = dff // FF_CHUNK
    g3 = wg.reshape(d, n, FF_CHUNK)
    u3 = wu.reshape(d, n, FF_CHUNK)
    wgu = jnp.transpose(jnp.concatenate([g3, u3], axis=2), (1, 0, 2)).astype(BF16)
    return wgu, wd.reshape(n, FF_CHUNK, d).astype(BF16)


def kernel(x_prompt, x_sample, cache_a_k, cache_a_v, state_b_c, state_b_n, state_b_m, state_b_conv, cache_c_k, cache_c_v, c_prompt, c_sample, w_in_ab, sink_a, conv_w_b, conv_b_b, b_gates_b, hnorm_b, w_out_ab, w_in_c, relbias_c, w_out_c, w_ada, b_ada, norm_g, w_ffn_gate, w_ffn_up, w_ffn_down):
    Bp, Lp, d = x_prompt.shape
    Bs, Ls, _ = x_sample.shape
    depth = w_ada.shape[0]
    tm_p = ROW_TILE
    assert Lp % ROW_TILE == 0 and Lp % (CHUNK * CHUNKS_PER_STEP_A) == 0 and Lp % (CHUNK * CHUNKS_PER_STEP_C) == 0
    assert (Bp + Bs) % 8 == 0 and Ls % 8 == 0 and Ls >= CONV_W - 1 and Bp % 2 == 0 and Bs % 2 == 0

    mods = _ada_call(jnp.concatenate([c_prompt, c_sample], axis=0), w_ada, b_ada)

    def mod_parts(i, lo, hi):
        m = mods[i, lo:hi].reshape(hi - lo, 1, 6, d)
        return [m[:, :, k, :] for k in range(6)]

    c_qa, c_kv, c_qkb = 0, W_A, W_A + 2 * KV_A
    c_vb = c_qkb + 2 * W_B
    c_og = c_vb + W_B
    c_gt = c_og + W_B
    groups_a = [(0, list(range(0, W_A, KV_A)))]
    wg_c = 4 * HEAD_DIM
    groups_c = [(o, [o]) for o in range(0, W_C, wg_c)]

    xp, xs = x_prompt, x_sample
    st_p = {k: [] for k in ("a_k", "a_v", "b_c", "b_n", "b_m", "b_conv", "c_k", "c_v")}
    st_s = {k: [] for k in st_p}

    for i in range(depth):
        j = i // 2
        ng = norm_g[i]
        g0 = ng[0:1]
        shp1, scp1, gtp1, shp2, scp2, gtp2 = mod_parts(i, 0, Bp)
        shs1, scs1, gts1, shs2, scs2, gts2 = mod_parts(i, Bp, Bp + Bs)
        wgu, wdn = _prep_ffn(w_ffn_gate[i], w_ffn_up[i], w_ffn_down[i])

        if i % 2 == 0:
            w_in = _prep_w_in_ab(w_in_ab[j])
            wo_parts = [_perm_heads(w_out_ab[j][:W_A], 0).astype(BF16), w_out_ab[j][W_A:].astype(BF16)]
            conv_w = conv_w_b[j]
            conv_b = conv_b_b[j].reshape(1, 2 * W_B)
            bg = jnp.pad(b_gates_b[j], (0, GATE_W - 2 * H_B)).reshape(1, GATE_W)
            hn = hnorm_b[j].reshape(1, W_B)

            segs_p = [
                (c_qa, W_A, BF16, False, 0),
                (c_kv, 2 * KV_A, BF16, True, WINDOW_A),
                (c_qkb, 2 * W_B, BF16, False, TAIL_ROWS),
                (c_vb, W_B, BF16, False, 0),
                (c_og, W_B, BF16, False, 0),
                (c_gt, GATE_W, F32, False, 0),
            ]
            qa, kva, qkb, vb, og, gts, kv_tail, conv_tail = _inproj_call(
                xp, scp1, shp1, g0, w_in, segs_p, tm_p, 1, "inproj_ab_p")
            out_a = _band_attn_call(qa, kva, _alibi_rows(sink_a[j], CHUNK), KV_A, groups_a, BAND_A, True,
                                    CHUNKS_PER_STEP_A, "attn_a_p")
            hb, s_c, s_n, s_m = _mlstm_call(qkb, vb, og, gts, conv_w, conv_b, bg, hn, None, CHUNK, "mlstm_p")
            xp = _outffn_call([out_a, hb], wo_parts, xp, gtp1, scp2, shp2, gtp2, ng, wgu, wdn, tm_p, "outffn_ab_p")
            st_p["a_k"].append(kv_tail[:, :, :KV_A].reshape(Bp, WINDOW_A, HKV_A, HEAD_DIM))
            st_p["a_v"].append(kv_tail[:, :, KV_A:].reshape(Bp, WINDOW_A, HKV_A, HEAD_DIM))
            st_p["b_c"].append(s_c)
            st_p["b_n"].append(s_n)
            st_p["b_m"].append(s_m[:, :, 0])
            st_p["b_conv"].append(conv_tail[:, TAIL_ROWS - (CONV_W - 1):, :])

            segs_s = [
                (c_qa, W_A, BF16, False, 0),
                (c_kv, 2 * KV_A, F32, False, 0),
                (c_qkb, 2 * W_B, BF16, False, TAIL_ROWS),
                (c_vb, W_B, BF16, False, 0),
                (c_og, W_B, BF16, False, 0),
                (c_gt, GATE_W, F32, False, 0),
            ]
            qa, kvn, qkb, vb, og, gts, conv_tail = _inproj_call(
                xs, scs1, shs1, g0, w_in, segs_s, Ls, 0, "inproj_ab_s")
            la = cache_a_k.shape[2]
            out_a = _cache_attn_call(qa, cache_a_k[j].reshape(Bs, la, KV_A), cache_a_v[j].reshape(Bs, la, KV_A),
                                     kvn, _alibi_rows(sink_a[j], Ls), KV_A, groups_a, True, "attn_a_s")
            init = (state_b_c[j], state_b_n[j],
                    jnp.broadcast_to(state_b_m[j][:, :, None], (Bs, H_B, GATE_W)),
                    jnp.pad(state_b_conv[j], ((0, 0), (TAIL_ROWS - (CONV_W - 1), 0), (0, 0))))
            hb, s_c, s_n, s_m = _mlstm_call(qkb, vb, og, gts, conv_w, conv_b, bg, hn, init, Ls, "mlstm_s")
            xs = _outffn_call([out_a, hb], wo_parts, xs, gts1, scs2, shs2, gts2, ng, wgu, wdn, Ls, "outffn_ab_s")
            st_s["a_k"].append(kvn[:, :, :KV_A].reshape(Bs, Ls, HKV_A, HEAD_DIM))
            st_s["a_v"].append(kvn[:, :, KV_A:].reshape(Bs, Ls, HKV_A, HEAD_DIM))
            st_s["b_c"].append(s_c)
            st_s["b_n"].append(s_n)
            st_s["b_m"].append(s_m[:, :, 0])
            st_s["b_conv"].append(conv_tail[:, TAIL_ROWS - (CONV_W - 1):, :])
        else:
            w_in = jnp.concatenate([w_in_c[j][:, :W_C] * QK_SCALE, w_in_c[j][:, W_C:]], axis=1).astype(BF16)
            wo_parts = [w_out_c[j].astype(BF16)]
            bias = _relbias_call(relbias_c[j])
            lc = cache_c_k.shape[2]
            tail_c = NPREV_C * CHUNK

            segs_p = [
                (0, W_C, BF16, False, 0),
                (W_C, 2 * W_C, BF16, True, tail_c),
            ]
            qc, kvc, kv_tail = _inproj_call(xp, scp1, shp1, g0, w_in, segs_p, tm_p, 1, "inproj_c_p")
            out_c = _band_attn_call(qc, kvc, bias.reshape(H_C * CHUNK, BAND_C), wg_c, groups_c, BAND_C, False,
                                    CHUNKS_PER_STEP_C, "attn_c_p")
            xp = _outffn_call([out_c], wo_parts, xp, gtp1, scp2, shp2, gtp2, ng, wgu, wdn, tm_p, "outffn_c_p")
            st_p["c_k"].append(kv_tail[:, :, :W_C].reshape(Bp, tail_c, H_C, HEAD_DIM))
            st_p["c_v"].append(kv_tail[:, :, W_C:].reshape(Bp, tail_c, H_C, HEAD_DIM))

            segs_s = [
                (0, W_C, BF16, False, 0),
                (W_C, 2 * W_C, F32, False, 0),
            ]
            qc, kvn = _inproj_call(xs, scs1, shs1, g0, w_in, segs_s, Ls, 0, "inproj_c_s")
            off = BAND_C - CHUNK - lc
            bias_c = bias[:, :Ls, off:off + lc].reshape(H_C * Ls, lc)
            bias_n = bias[:, :Ls, off + lc:off + lc + Ls].reshape(H_C * Ls, Ls)
            out_c = _cache_attn_call(qc, cache_c_k[j].reshape(Bs, lc, W_C), cache_c_v[j].reshape(Bs, lc, W_C),
                                     kvn, (bias_c, bias_n), wg_c, groups_c, False, "attn_c_s")
            xs = _outffn_call([out_c], wo_parts, xs, gts1, scs2, shs2, gts2, ng, wgu, wdn, Ls, "outffn_c_s")
            st_s["c_k"].append(kvn[:, :, :W_C].reshape(Bs, Ls, H_C, HEAD_DIM))
            st_s["c_v"].append(kvn[:, :, W_C:].reshape(Bs, Ls, H_C, HEAD_DIM))

    order = ("a_k", "a_v", "b_c", "b_n", "b_m", "b_conv", "c_k", "c_v")
    outs = [xp, xs]
    outs += [jnp.stack(st_p[k]) for k in order]
    outs += [jnp.stack(st_s[k]) for k in order]
    return tuple(outs)
```

```python
import jax
import jax.numpy as jnp
from jax import lax
from jax.experimental import pallas as pl
from jax.experimental.pallas import tpu as pltpu

F32 = jnp.float32
BF16 = jnp.bfloat16

D_MODEL = 1024
CHUNK = 64
HEAD_DIM = 64
HQ_A = 8
HKV_A = 2
WINDOW_A = 128
H_B = 4
DH_B = 128
CONV_W = 4
H_C = 16
NPREV_C = 8
MAX_REL_C = 256
EPS = 1e-6
NEG = -1e30
W_A = HQ_A * HEAD_DIM
KV_A = HKV_A * HEAD_DIM
W_B = H_B * DH_B
W_C = H_C * HEAD_DIM
BAND_A = WINDOW_A + CHUNK
BAND_C = (NPREV_C + 1) * CHUNK
FF_CHUNK = 256
GATE_W = 128
TAIL_ROWS = 8
ROW_TILE = 512
V7X_VMEM_LIMIT = 56 * 1024 * 1024
QK_SCALE = HEAD_DIM ** -0.5
PERM_A = tuple(h for pair in zip(range(HQ_A // HKV_A), range(HQ_A // HKV_A, HQ_A)) for h in pair)
CHUNKS_PER_STEP_A = 8
CHUNKS_PER_STEP_C = 2
PHASE_UNITS_A = 8
PHASE_UNITS_C = 1
MLSTM_SEQS_PER_STEP = 4


def _params(sem, vmem=V7X_VMEM_LIMIT, flags=None):
    return pltpu.CompilerParams(dimension_semantics=sem, vmem_limit_bytes=vmem, flags=flags)


def _const_spec(shape):
    nd = len(shape)
    return pl.BlockSpec(shape, lambda *_: (0,) * nd, pipeline_mode=pl.Buffered(1))


def _split3(x):
    hi = x.astype(BF16)
    r1 = x - hi.astype(F32)
    mid = r1.astype(BF16)
    lo = (r1 - mid.astype(F32)).astype(BF16)
    return hi, mid, lo


def _split3_f32(x):
    return tuple(p.astype(F32) for p in _split3(x))


def _dot(a, b):
    return jnp.dot(a, b, preferred_element_type=F32)


def _dot_nt(a, b):
    return lax.dot_general(a, b, (((1,), (1,)), ((), ())), preferred_element_type=F32)


def _rms(x, g):
    return x * lax.rsqrt(jnp.mean(x * x, axis=-1, keepdims=True) + EPS) * g


def _sigmoid(x):
    return 1.0 / (1.0 + jnp.exp(-x))


def _log_sigmoid(x):
    return jnp.minimum(x, 0.0) - jnp.log(1.0 + jnp.exp(-jnp.abs(x)))


def _ada_kernel(c_ref, w_ref, b_ref, o_ref):
    c = c_ref[...]
    s = c * _sigmoid(c)
    s_hi, s_mid, _ = _split3(s)
    w_hi, w_mid, _ = _split3(w_ref[0])
    acc = _dot(s_hi, w_hi) + _dot(s_hi, w_mid) + _dot(s_mid, w_hi)
    o_ref[0] = acc + b_ref[0]


def _ada_call(c_all, w_ada, b_ada):
    depth, d, n6 = w_ada.shape
    nb = c_all.shape[0]
    tn = 512
    return pl.pallas_call(
        _ada_kernel,
        grid=(depth, n6 // tn),
        in_specs=[
            pl.BlockSpec((nb, d), lambda i, j: (0, 0)),
            pl.BlockSpec((1, d, tn), lambda i, j: (i, 0, j)),
            pl.BlockSpec((1, 1, tn), lambda i, j: (i, 0, j)),
        ],
        out_specs=pl.BlockSpec((1, nb, tn), lambda i, j: (i, 0, j)),
        out_shape=jax.ShapeDtypeStruct((depth, nb, n6), F32),
        compiler_params=_params(("arbitrary", "arbitrary")),
        name="ada_mod",
    )(c_all, w_ada, b_ada.reshape(depth, 1, n6))


def _inproj_call(x, sc, sh, g, w, segs, tm, pad, name, bb=1):
    B, L, d = x.shape
    nT = L // tm
    P = w.shape[1]
    n_seg = len(segs)

    def kernel(x_ref, sc_ref, sh_ref, g_ref, w_ref, *outs):
        seg_refs = outs[:n_seg]
        tail_refs = outs[n_seg:]
        t = pl.program_id(1)

        def compute():
            h = (_rms(x_ref[...], g_ref[...]) * (1.0 + sc_ref[...]) + sh_ref[...]).astype(BF16)
            h = h.reshape(bb * tm, d)
            ti = 0
            for (c0, wd, dt, _, tail_rows, tail_split), r in zip(segs, seg_refs):
                y = _dot(h, w_ref[:, c0:c0 + wd]).reshape(bb, tm, wd)
                r[...] = y.astype(dt)
                if tail_rows:
                    trs = tail_refs[ti:ti + tail_split]
                    ti += tail_split
                    ws = wd // tail_split

                    @pl.when(t == nT - 1 + pad)
                    def _():
                        for k, tr in enumerate(trs):
                            tr[...] = y[:, tm - tail_rows:, k * ws:(k + 1) * ws]

        if pad:
            @pl.when(t == 0)
            def _():
                for (_, wd, dt, padded, _, _), r in zip(segs, seg_refs):
                    if padded:
                        r[...] = jnp.zeros((bb, tm, wd), dt)

            pl.when(t > 0)(compute)
        else:
            compute()

    def row_idx(t):
        return jnp.maximum(t - 1, 0) if pad else t

    in_specs = [
        pl.BlockSpec((bb, tm, d), lambda b, t: (b, row_idx(t), 0)),
        pl.BlockSpec((bb, 1, d), lambda b, t: (b, 0, 0)),
        pl.BlockSpec((bb, 1, d), lambda b, t: (b, 0, 0)),
        _const_spec((1, d)),
        _const_spec((d, P)),
    ]
    out_specs, out_shapes = [], []
    for (_, wd, dt, padded, _, _) in segs:
        if padded:
            out_specs.append(pl.BlockSpec((bb, tm, wd), lambda b, t: (b, t, 0)))
            out_shapes.append(jax.ShapeDtypeStruct((B, tm + L, wd), dt))
        else:
            out_specs.append(pl.BlockSpec((bb, tm, wd), lambda b, t: (b, row_idx(t), 0)))
            out_shapes.append(jax.ShapeDtypeStruct((B, L, wd), dt))
    for (_, wd, _, _, tail_rows, tail_split) in segs:
        for _ in range(tail_split if tail_rows else 0):
            out_specs.append(pl.BlockSpec((bb, tail_rows, wd // tail_split), lambda b, t: (b, 0, 0)))
            out_shapes.append(jax.ShapeDtypeStruct((B, tail_rows, wd // tail_split), F32))

    return pl.pallas_call(
        kernel,
        grid=(B // bb, nT + pad),
        in_specs=in_specs,
        out_specs=out_specs,
        out_shape=out_shapes,
        compiler_params=_params(("arbitrary", "arbitrary")),
        name=name,
    )(x, sc, sh, g, w)


def _attend_grouped(q_tiles, pieces, sink_col, lq):
    return _attend_units([(q_tiles, pieces, sink_col)], lq)[0]


def _attend_units(units, lq):
    wg = units[0][0][0].shape[1]
    nb = wg // HEAD_DIM
    blk = lax.shift_right_logical(lax.broadcasted_iota(jnp.int32, (lq, wg), 1), HEAD_DIM.bit_length() - 1)
    scores = []
    for q_tiles, pieces, _ in units:
        rows = []
        for qt in q_tiles:
            for r in range(nb):
                rows.append(jnp.where(blk == r, qt, 0.0).astype(BF16))
        qbd = jnp.concatenate(rows, axis=0)
        ss = []
        for k, _, bias, valid in pieces:
            s = _dot_nt(qbd, k) + bias
            if valid is not None:
                s = jnp.where(valid, s, NEG)
            ss.append(s)
        scores.append(ss)
    maxima = []
    for ss, (_, _, sink_col) in zip(scores, units):
        m = ss[0].max(axis=-1, keepdims=True)
        for s in ss[1:]:
            m = jnp.maximum(m, s.max(axis=-1, keepdims=True))
        if sink_col is not None:
            m = jnp.maximum(m, sink_col)
        maxima.append(m)
    probs, dens = [], []
    for ss, m, (_, _, sink_col) in zip(scores, maxima, units):
        ps = [jnp.exp(s - m) for s in ss]
        den = ps[0].sum(axis=-1, keepdims=True)
        for p in ps[1:]:
            den = den + p.sum(axis=-1, keepdims=True)
        if sink_col is not None:
            den = den + jnp.exp(sink_col - m)
        probs.append([p.astype(BF16) for p in ps])
        dens.append(den)
    results = []
    for ps, den, (q_tiles, pieces, _) in zip(probs, dens, units):
        o = _dot(ps[0], pieces[0][1])
        for p, piece in zip(ps[1:], pieces[1:]):
            o = o + _dot(p, piece[1])
        o = o / den
        outs = []
        for t in range(len(q_tiles)):
            base = t * nb * lq
            acc = o[base + (nb - 1) * lq:base + nb * lq]
            for r in range(nb - 2, -1, -1):
                acc = jnp.where(blk == r, o[base + r * lq:base + (r + 1) * lq], acc)
            outs.append(acc)
        results.append(outs)
    return results


def _alibi_rows(sinks, lq):
    perm = jnp.array(PERM_A)
    slopes = jnp.exp2(-8.0 * jnp.arange(1, HQ_A + 1, dtype=F32) / HQ_A)
    rc = jnp.stack([-slopes[perm], sinks.astype(F32)[perm]], axis=1)
    return jnp.repeat(rc, lq, axis=0)


def _band_attn_call(q, kv, extra, wg, q_groups, band, alibi, nq, phase_units, name):
    B, L, wq = q.shape
    nsteps = L // (CHUNK * nq)
    hist = band - CHUNK
    kvw = kv.shape[2]
    v_off = kvw // 2
    nb = wg // HEAD_DIM

    def kernel(q_ref, kv_ref, e_ref, o_ref):
        step = pl.program_id(1)
        jj = lax.broadcasted_iota(jnp.int32, (1, band), 1)
        if alibi:
            m_rows = len(q_groups[0][1]) * nb * CHUNK
            ii = lax.broadcasted_iota(jnp.int32, (m_rows, band), 0) & (CHUNK - 1)
            dist = jnp.abs(ii + hist - lax.broadcasted_iota(jnp.int32, (m_rows, band), 1)).astype(F32)
        rows_of = []
        bias_of = []
        for _, q_offs in q_groups:
            r0 = (q_offs[0] // HEAD_DIM) * CHUNK
            r1 = r0 + len(q_offs) * nb * CHUNK
            rows_of.append((r0, r1))
            bias_of.append(e_ref[r0:r1, 0:1] * dist if alibi else None)
        units, dests = [], []
        for cc in range(nq):
            c = step * nq + cc
            start = pl.multiple_of(c * CHUNK + ROW_TILE - hist, CHUNK)
            valid = (c * CHUNK - hist + jj) >= 0
            rs = slice(cc * CHUNK, (cc + 1) * CHUNK)
            for gi, (kv0, q_offs) in enumerate(q_groups):
                kb = kv_ref[0, pl.ds(start, band), kv0:kv0 + wg]
                vb = kv_ref[0, pl.ds(start, band), v_off + kv0:v_off + kv0 + wg]
                tiles = [q_ref[0, rs, o:o + wg].astype(F32) for o in q_offs]
                r0, r1 = rows_of[gi]
                if alibi:
                    units.append((tiles, [(kb, vb, bias_of[gi], valid)], e_ref[r0:r1, 1:2]))
                else:
                    units.append((tiles, [(kb, vb, e_ref[r0:r1, :], valid)], None))
                dests.append((rs, q_offs))
        for u0 in range(0, len(units), phase_units):
            for (rs, q_offs), outs in zip(dests[u0:u0 + phase_units],
                                          _attend_units(units[u0:u0 + phase_units], CHUNK)):
                for o, res in zip(q_offs, outs):
                    o_ref[0, rs, o:o + wg] = res.astype(BF16)

    return pl.pallas_call(
        kernel,
        grid=(B, nsteps),
        in_specs=[
            pl.BlockSpec((1, nq * CHUNK, wq), lambda b, c: (b, c, 0)),
            pl.BlockSpec((1, ROW_TILE + L, kvw), lambda b, c: (b, 0, 0)),
            _const_spec(extra.shape),
        ],
        out_specs=pl.BlockSpec((1, nq * CHUNK, wq), lambda b, c: (b, c, 0)),
        out_shape=jax.ShapeDtypeStruct((B, L, wq), BF16),
        compiler_params=_params(("arbitrary", "arbitrary")),
        name=name,
    )(q, kv, extra)


def _cache_attn_call(q, kc, vc, kvn, extra, wg, q_groups, alibi, name):
    B, T, wq = q.shape
    Lc = kc.shape[1]
    v_off = kvn.shape[2] // 2
    nb = wg // HEAD_DIM
    assert T & (T - 1) == 0

    def kernel(q_ref, kc_ref, vc_ref, kvn_ref, *rest):
        m_rows = len(q_groups[0][1]) * nb * T
        if alibi:
            e_ref, o_ref = rest
            ii = lax.broadcasted_iota(jnp.int32, (m_rows, Lc), 0) & (T - 1)
            dist_c = jnp.abs(ii + Lc - lax.broadcasted_iota(jnp.int32, (m_rows, Lc), 1)).astype(F32)
            ii = lax.broadcasted_iota(jnp.int32, (m_rows, T), 0) & (T - 1)
            dist_n = jnp.abs(ii - lax.broadcasted_iota(jnp.int32, (m_rows, T), 1)).astype(F32)
        else:
            bc_ref, bn_ref, o_ref = rest
        for kv0, q_offs in q_groups:
            k_c = kc_ref[0, :, kv0:kv0 + wg].astype(BF16)
            v_c = vc_ref[0, :, kv0:kv0 + wg].astype(BF16)
            k_n = kvn_ref[0, :, kv0:kv0 + wg].astype(BF16)
            v_n = kvn_ref[0, :, v_off + kv0:v_off + kv0 + wg].astype(BF16)
            tiles = [q_ref[0, :, o:o + wg].astype(F32) for o in q_offs]
            r0 = (q_offs[0] // HEAD_DIM) * T
            r1 = r0 + m_rows
            if alibi:
                slope = e_ref[r0:r1, 0:1]
                outs = _attend_grouped(tiles, [(k_c, v_c, slope * dist_c, None), (k_n, v_n, slope * dist_n, None)],
                                       e_ref[r0:r1, 1:2], T)
            else:
                outs = _attend_grouped(tiles, [(k_c, v_c, bc_ref[r0:r1, :], None),
                                               (k_n, v_n, bn_ref[r0:r1, :], None)], None, T)
            for o, res in zip(q_offs, outs):
                o_ref[0, :, o:o + wg] = res.astype(BF16)

    in_specs = [
        pl.BlockSpec((1, T, wq), lambda b: (b, 0, 0)),
        pl.BlockSpec((1, Lc, kc.shape[2]), lambda b: (b, 0, 0)),
        pl.BlockSpec((1, Lc, vc.shape[2]), lambda b: (b, 0, 0)),
        pl.BlockSpec((1, T, kvn.shape[2]), lambda b: (b, 0, 0)),
    ]
    if alibi:
        in_specs.append(_const_spec(extra.shape))
        args = (q, kc, vc, kvn, extra)
    else:
        in_specs += [_const_spec(extra[0].shape), _const_spec(extra[1].shape)]
        args = (q, kc, vc, kvn, extra[0], extra[1])
    return pl.pallas_call(
        kernel,
        grid=(B,),
        in_specs=in_specs,
        out_specs=pl.BlockSpec((1, T, wq), lambda b: (b, 0, 0)),
        out_shape=jax.ShapeDtypeStruct((B, T, wq), BF16),
        compiler_params=_params(("arbitrary",)),
        name=name,
    )(*args)


def _relbias_kernel(tab_ref, o_ref):
    nt = tab_ref.shape[1]
    hi, mid, lo = _split3(tab_ref[...])
    tt = lax.broadcasted_iota(jnp.int32, (nt, BAND_C), 0)
    jj = lax.broadcasted_iota(jnp.int32, (nt, BAND_C), 1)

    def body(i, carry):
        idx = jnp.clip(i - jj + NPREV_C * CHUNK, -MAX_REL_C, MAX_REL_C) + MAX_REL_C
        oh = jnp.where(tt == idx, 1.0, 0.0).astype(BF16)
        o_ref[i] = _dot(hi, oh) + _dot(mid, oh) + _dot(lo, oh)
        return carry

    lax.fori_loop(0, CHUNK, body, 0)


def _relbias_call(table):
    nh, nt = table.shape
    ntp = ((nt + 127) // 128) * 128
    tab = jnp.pad(table, ((0, 0), (0, ntp - nt)))
    out = pl.pallas_call(
        _relbias_kernel,
        out_shape=jax.ShapeDtypeStruct((CHUNK, nh, BAND_C), F32),
        compiler_params=_params(None),
        name="relbias",
    )(tab)
    return jnp.transpose(out, (1, 0, 2))


def _mlstm_call(qkb, vb, og, gates, conv_w, conv_b, bg, hnorm, init, Lc, name, bb=MLSTM_SEQS_PER_STEP):
    B, L, _ = qkb.shape
    nC = L // Lc
    zero_init = init is None
    k_scale = DH_B ** -0.5

    def kernel(*refs):
        if zero_init:
            (qkb_ref, vb_ref, og_ref, gt_ref, cw_ref, cb_ref, bg_ref, hn_ref,
             hb_ref, C_ref, n_ref, m_ref, tail_ref, ext_ref) = refs
        else:
            (qkb_ref, vb_ref, og_ref, gt_ref, cw_ref, cb_ref, bg_ref, hn_ref,
             C0_ref, n0_ref, m0_ref, conv0_ref,
             hb_ref, C_ref, n_ref, m_ref, tail_ref, ext_ref) = refs
        c = pl.program_id(1)

        @pl.when(c == 0)
        def _():
            if zero_init:
                C_ref[...] = jnp.zeros(C_ref.shape, F32)
                n_ref[...] = jnp.zeros(n_ref.shape, F32)
                m_ref[...] = jnp.zeros(m_ref.shape, F32)
                tail_ref[...] = jnp.zeros(tail_ref.shape, F32)
            else:
                C_ref[...] = C0_ref[...]
                n_ref[...] = n0_ref[...]
                m_ref[...] = m0_ref[...]
                tail_ref[...] = conv0_ref[...]

        r_i = lax.broadcasted_iota(jnp.int32, (Lc, Lc), 0)
        c_i = lax.broadcasted_iota(jnp.int32, (Lc, Lc), 1)
        tri = r_i >= c_i
        tri_bf = jnp.where(tri, 1.0, 0.0).astype(BF16)
        r_u = lax.broadcasted_iota(jnp.int32, (GATE_W, GATE_W), 0)
        c_u = lax.broadcasted_iota(jnp.int32, (GATE_W, GATE_W), 1)
        upper_bf = jnp.where(r_u <= c_u, 1.0, 0.0).astype(BF16)
        cw = cw_ref[...]
        cb = cb_ref[...]
        bgv = bg_ref[...]
        hn = hn_ref[...]

        gbs, g_ts, qks, l_parts, t_parts = [], [], [], [], []
        for b in range(bb):
            u = qkb_ref[b].astype(F32)
            ext_ref[b, 0:TAIL_ROWS, :] = tail_ref[b]
            ext_ref[b, TAIL_ROWS:TAIL_ROWS + Lc, :] = u
            y = cb + cw[CONV_W - 1:CONV_W] * u
            for j in range(CONV_W - 1):
                y = y + cw[j:j + 1] * ext_ref[b, pl.ds(TAIL_ROWS - (CONV_W - 1) + j, Lc), :]
            tail_ref[b] = ext_ref[b, Lc:Lc + TAIL_ROWS, :]
            qks.append(y * _sigmoid(y))

            gb = gt_ref[b] + bgv
            gbs.append(gb)
            l_parts += list(_split3_f32(_log_sigmoid(gb)))
            g_pad = jnp.concatenate([gb, jnp.zeros((GATE_W - Lc, GATE_W), F32)], axis=0)
            g_t = g_pad.T
            g_ts.append(g_t)
            t_parts += list(_split3_f32(_log_sigmoid(g_t[0:2 * H_B, :])))
        col_sums = _dot(tri_bf, jnp.concatenate(l_parts, axis=1).astype(BF16))
        row_sums = _dot(jnp.concatenate(t_parts, axis=0).astype(BF16), upper_bf)

        def body(b):
            gb = gbs[b]
            qk = qks[b]
            c0 = 3 * b * GATE_W
            b_col = (col_sums[:, c0:c0 + GATE_W] + col_sums[:, c0 + GATE_W:c0 + 2 * GATE_W]
                     + col_sums[:, c0 + 2 * GATE_W:c0 + 3 * GATE_W])
            r0 = 3 * b * 2 * H_B
            b_row = row_sums[r0:r0 + 2 * H_B] + row_sums[r0 + 2 * H_B:r0 + 4 * H_B] + row_sums[r0 + 4 * H_B:r0 + 6 * H_B]
            a_row_all = g_ts[b][0:H_B, :] - b_row[H_B:2 * H_B, :]

            ogv = og_ref[b].astype(F32)
            vv = vb_ref[b]
            for h in range(H_B):
                hs = slice(h * DH_B, (h + 1) * DH_B)
                qh = qk[:, hs]
                kh = qk[:, W_B + h * DH_B:W_B + (h + 1) * DH_B] * k_scale
                vh = vv[:, hs]
                qb = qh.astype(BF16)
                kb = kh.astype(BF16)
                a_row = a_row_all[h:h + 1, 0:Lc]
                bcol = b_col[:, H_B + h:H_B + h + 1]
                a_col = gb[:, h:h + 1] - bcol
                m_prev = m_ref[b, h:h + 1, 0:1]
                amat = jnp.where(tri, a_row, -jnp.inf)
                mc = jnp.maximum(amat.max(axis=-1, keepdims=True), m_prev)
                dm = jnp.exp(amat - mc)
                c_h = C_ref[b, h]
                qck = _dot_nt(qb, jnp.concatenate([c_h.astype(BF16), kb], axis=0))
                w = qck[:, DH_B:DH_B + Lc] * dm
                a_int = jnp.exp(m_prev - mc)
                num = a_int * qck[:, 0:DH_B] + _dot(w.astype(BF16), vh)
                n_h = n_ref[b, h:h + 1, :]
                den = a_int * jnp.sum(qh * n_h, axis=-1, keepdims=True) + w.sum(axis=-1, keepdims=True)
                hh = num / jnp.maximum(jnp.abs(den), jnp.exp(-(bcol + mc)))
                hh = _rms(hh, hn[:, hs]) * _sigmoid(ogv[:, hs])
                hb_ref[b, :, hs] = hh.astype(BF16)

                m_end = mc[Lc - 1:Lc, :]
                a_end = jnp.exp(m_prev - m_end)
                w_end = jnp.exp(a_col - m_end)
                vw = vh.astype(F32) * w_end
                zpad = jnp.zeros((DH_B - Lc, DH_B), F32)
                vw_t = jnp.concatenate([vw, zpad], axis=0).T.astype(BF16)
                k_p = jnp.concatenate([kh, zpad], axis=0).astype(BF16)
                C_ref[b, h] = a_end * c_h + _dot(vw_t, k_p)
                n_ref[b, h:h + 1, :] = a_end * n_h + jnp.sum(kh * w_end, axis=0, keepdims=True)
                m_ref[b, h:h + 1, :] = jnp.broadcast_to(bcol[Lc - 1:Lc, :] + m_end, (1, GATE_W))

        for b in range(bb):
            body(b)

    w2 = 2 * W_B
    in_specs = [
        pl.BlockSpec((bb, Lc, w2), lambda i, c: (i, c, 0)),
        pl.BlockSpec((bb, Lc, W_B), lambda i, c: (i, c, 0)),
        pl.BlockSpec((bb, Lc, W_B), lambda i, c: (i, c, 0)),
        pl.BlockSpec((bb, Lc, GATE_W), lambda i, c: (i, c, 0)),
        _const_spec((CONV_W, w2)),
        _const_spec((1, w2)),
        _const_spec((1, GATE_W)),
        _const_spec((1, W_B)),
    ]
    args = [qkb, vb, og, gates, conv_w, conv_b, bg, hnorm]
    if not zero_init:
        in_specs += [
            pl.BlockSpec((bb, H_B, DH_B, DH_B), lambda i, c: (i, 0, 0, 0)),
            pl.BlockSpec((bb, H_B, DH_B), lambda i, c: (i, 0, 0)),
            pl.BlockSpec((bb, H_B, GATE_W), lambda i, c: (i, 0, 0)),
            pl.BlockSpec((bb, TAIL_ROWS, w2), lambda i, c: (i, 0, 0)),
        ]
        args += list(init)
    out_specs = [
        pl.BlockSpec((bb, Lc, W_B), lambda i, c: (i, c, 0)),
        pl.BlockSpec((bb, H_B, DH_B, DH_B), lambda i, c: (i, 0, 0, 0)),
        pl.BlockSpec((bb, H_B, DH_B), lambda i, c: (i, 0, 0)),
        pl.BlockSpec((bb, H_B, GATE_W), lambda i, c: (i, 0, 0)),
    ]
    out_shape = [
        jax.ShapeDtypeStruct((B, L, W_B), BF16),
        jax.ShapeDtypeStruct((B, H_B, DH_B, DH_B), F32),
        jax.ShapeDtypeStruct((B, H_B, DH_B), F32),
        jax.ShapeDtypeStruct((B, H_B, GATE_W), F32),
    ]
    return pl.pallas_call(
        kernel,
        grid=(B // bb, nC),
        in_specs=in_specs,
        out_specs=out_specs,
        out_shape=out_shape,
        scratch_shapes=[pltpu.VMEM((bb, TAIL_ROWS, w2), F32), pltpu.VMEM((bb, TAIL_ROWS + Lc, w2), F32)],
        compiler_params=_params(("arbitrary", "arbitrary")),
        name=name,
    )(*args)


def _outffn_call(parts, wo_parts, x, gt1, sc2, sh2, gt2, ng, wgu, wd, tm, name, bb=1):
    B, L, d = x.shape
    nT = L // tm
    n_in = len(parts)
    n_ff = wgu.shape[0]
    fc = wd.shape[1]
    rows = bb * tm

    def kernel(*refs):
        a_refs = refs[:n_in]
        wo_refs = refs[n_in:2 * n_in]
        (x_ref, gt1_ref, sc2_ref, sh2_ref, gt2_ref, ng_ref, wgu_ref, wd_ref,
         o_ref, h_ref, z_ref) = refs[2 * n_in:]
        y = None
        for a_ref, wo_ref in zip(a_refs, wo_refs):
            part = _dot(a_ref[...].reshape(rows, a_ref.shape[2]), wo_ref[...])
            y = part if y is None else y + part
        x1 = x_ref[...] + gt1_ref[...] * _rms(y, ng_ref[1:2, :]).reshape(bb, tm, d)
        o_ref[...] = x1
        h = _rms(x1, ng_ref[2:3, :]) * (1.0 + sc2_ref[...]) + sh2_ref[...]
        h_ref[...] = h.reshape(rows, d).astype(BF16)

        def body(j, carry):
            gu = _dot(h_ref[...], wgu_ref[j])
            gpart = gu[:, :fc]
            z_ref[j] = (gpart * _sigmoid(gpart) * gu[:, fc:]).astype(BF16)
            return carry

        lax.fori_loop(0, n_ff, body, 0, unroll=True)
        acc = _dot(z_ref[0], wd_ref[0])
        for j in range(1, n_ff):
            acc = acc + _dot(z_ref[j], wd_ref[j])
        o_ref[...] = o_ref[...] + gt2_ref[...] * _rms(acc, ng_ref[3:4, :]).reshape(bb, tm, d)

    mod_spec = pl.BlockSpec((bb, 1, d), lambda b, t: (b, 0, 0))
    in_specs = [pl.BlockSpec((bb, tm, p.shape[2]), lambda b, t: (b, t, 0)) for p in parts]
    in_specs += [_const_spec(w.shape) for w in wo_parts]
    in_specs += [
        pl.BlockSpec((bb, tm, d), lambda b, t: (b, t, 0)),
        mod_spec, mod_spec, mod_spec, mod_spec,
        _const_spec(ng.shape),
        _const_spec(wgu.shape),
        _const_spec(wd.shape),
    ]
    return pl.pallas_call(
        kernel,
        grid=(B // bb, nT),
        in_specs=in_specs,
        out_specs=pl.BlockSpec((bb, tm, d), lambda b, t: (b, t, 0)),
        out_shape=jax.ShapeDtypeStruct((B, L, d), F32),
        scratch_shapes=[pltpu.VMEM((rows, d), BF16), pltpu.VMEM((n_ff, rows, fc), BF16)],
        compiler_params=_params(("arbitrary", "arbitrary")),
        name=name,
    )(*parts, *wo_parts, x, gt1, sc2, sh2, gt2, ng, wgu, wd)


def _perm_heads(w, axis):
    blocks = [lax.slice_in_dim(w, h * HEAD_DIM, (h + 1) * HEAD_DIM, axis=axis) for h in PERM_A]
    return jnp.concatenate(blocks, axis=axis)


def _prep_w_in_ab(w):
    o_k = W_A
    o_qk = W_A + 2 * KV_A
    o_vb = o_qk + 2 * W_B
    o_g = o_vb + W_B
    o_og = o_g + 2 * H_B
    d = w.shape[0]
    cols = [_perm_heads(w[:, :W_A], 1) * QK_SCALE, w[:, o_k:o_qk], w[:, o_qk:o_vb], w[:, o_vb:o_g],
            w[:, o_og:o_og + W_B], w[:, o_g:o_og], jnp.zeros((d, GATE_W - 2 * H_B), w.dtype)]
    return jnp.concatenate(cols, axis=1).astype(BF16)


def _prep_ffn(wg, wu, wd):
    d, dff = wg.shape
    n = dff // FF_CHUNK
    g3 = wg.reshape(d, n, FF_CHUNK)
    u3 = wu.reshape(d, n, FF_CHUNK)
    wgu = jnp.transpose(jnp.concatenate([g3, u3], axis=2), (1, 0, 2)).astype(BF16)
    return wgu, wd.reshape(n, FF_CHUNK, d).astype(BF16)


def kernel(x_prompt, x_sample, cache_a_k, cache_a_v, state_b_c, state_b_n, state_b_m, state_b_conv, cache_c_k, cache_c_v, c_prompt, c_sample, w_in_ab, sink_a, conv_w_b, conv_b_b, b_gates_b, hnorm_b, w_out_ab, w_in_c, relbias_c, w_out_c, w_ada, b_ada, norm_g, w_ffn_gate, w_ffn_up, w_ffn_down):
    Bp, Lp, d = x_prompt.shape
    Bs, Ls, _ = x_sample.shape
    depth = w_ada.shape[0]
    tm_p = ROW_TILE
    assert Lp % ROW_TILE == 0 and Lp % (CHUNK * CHUNKS_PER_STEP_A) == 0 and Lp % (CHUNK * CHUNKS_PER_STEP_C) == 0
    assert (Bp + Bs) % 8 == 0 and Ls % 8 == 0 and Ls >= CONV_W - 1 and Bp % MLSTM_SEQS_PER_STEP == 0 and Bs % MLSTM_SEQS_PER_STEP == 0

    mods = _ada_call(jnp.concatenate([c_prompt, c_sample], axis=0), w_ada, b_ada)

    def mod_parts(i, lo, hi):
        m = mods[i, lo:hi].reshape(hi - lo, 1, 6, d)
        return [m[:, :, k, :] for k in range(6)]

    c_qa, c_kv, c_qkb = 0, W_A, W_A + 2 * KV_A
    c_vb = c_qkb + 2 * W_B
    c_og = c_vb + W_B
    c_gt = c_og + W_B
    groups_a = [(0, list(range(0, W_A, KV_A)))]
    wg_c = 4 * HEAD_DIM
    groups_c = [(o, [o]) for o in range(0, W_C, wg_c)]

    xp, xs = x_prompt, x_sample
    st_p = {k: [] for k in ("a_k", "a_v", "b_c", "b_n", "b_m", "b_conv", "c_k", "c_v")}
    st_s = {k: [] for k in st_p}

    for i in range(depth):
        j = i // 2
        ng = norm_g[i]
        g0 = ng[0:1]
        shp1, scp1, gtp1, shp2, scp2, gtp2 = mod_parts(i, 0, Bp)
        shs1, scs1, gts1, shs2, scs2, gts2 = mod_parts(i, Bp, Bp + Bs)
        wgu, wdn = _prep_ffn(w_ffn_gate[i], w_ffn_up[i], w_ffn_down[i])

        if i % 2 == 0:
            w_in = _prep_w_in_ab(w_in_ab[j])
            wo_parts = [_perm_heads(w_out_ab[j][:W_A], 0).astype(BF16), w_out_ab[j][W_A:].astype(BF16)]
            conv_w = conv_w_b[j]
            conv_b = conv_b_b[j].reshape(1, 2 * W_B)
            bg = jnp.pad(b_gates_b[j], (0, GATE_W - 2 * H_B)).reshape(1, GATE_W)
            hn = hnorm_b[j].reshape(1, W_B)

            segs_p = [
                (c_qa, W_A, BF16, False, 0, 1),
                (c_kv, 2 * KV_A, BF16, True, WINDOW_A, 2),
                (c_qkb, 2 * W_B, BF16, False, TAIL_ROWS, 1),
                (c_vb, W_B, BF16, False, 0, 1),
                (c_og, W_B, BF16, False, 0, 1),
                (c_gt, GATE_W, F32, False, 0, 1),
            ]
            qa, kva, qkb, vb, og, gts, k_tail, v_tail, conv_tail = _inproj_call(
                xp, scp1, shp1, g0, w_in, segs_p, tm_p, 1, "inproj_ab_p")
            out_a = _band_attn_call(qa, kva, _alibi_rows(sink_a[j], CHUNK), KV_A, groups_a, BAND_A, True,
                                    CHUNKS_PER_STEP_A, PHASE_UNITS_A, "attn_a_p")
            hb, s_c, s_n, s_m = _mlstm_call(qkb, vb, og, gts, conv_w, conv_b, bg, hn, None, CHUNK, "mlstm_p")
            xp = _outffn_call([out_a, hb], wo_parts, xp, gtp1, scp2, shp2, gtp2, ng, wgu, wdn, tm_p, "outffn_ab_p")
            st_p["a_k"].append(k_tail.reshape(Bp, WINDOW_A, HKV_A, HEAD_DIM))
            st_p["a_v"].append(v_tail.reshape(Bp, WINDOW_A, HKV_A, HEAD_DIM))
            st_p["b_c"].append(s_c)
            st_p["b_n"].append(s_n)
            st_p["b_m"].append(s_m[:, :, 0])
            st_p["b_conv"].append(conv_tail[:, TAIL_ROWS - (CONV_W - 1):, :])

            segs_s = [
                (c_qa, W_A, BF16, False, 0, 1),
                (c_kv, 2 * KV_A, F32, False, 0, 1),
                (c_qkb, 2 * W_B, BF16, False, TAIL_ROWS, 1),
                (c_vb, W_B, BF16, False, 0, 1),
                (c_og, W_B, BF16, False, 0, 1),
                (c_gt, GATE_W, F32, False, 0, 1),
            ]
            qa, kvn, qkb, vb, og, gts, conv_tail = _inproj_call(
                xs, scs1, shs1, g0, w_in, segs_s, Ls, 0, "inproj_ab_s", bb=Bs)
            la = cache_a_k.shape[2]
            out_a = _cache_attn_call(qa, cache_a_k[j].reshape(Bs, la, KV_A), cache_a_v[j].reshape(Bs, la, KV_A),
                                     kvn, _alibi_rows(sink_a[j], Ls), KV_A, groups_a, True, "attn_a_s")
            init = (state_b_c[j], state_b_n[j],
                    jnp.broadcast_to(state_b_m[j][:, :, None], (Bs, H_B, GATE_W)),
                    jnp.pad(state_b_conv[j], ((0, 0), (TAIL_ROWS - (CONV_W - 1), 0), (0, 0))))
            hb, s_c, s_n, s_m = _mlstm_call(qkb, vb, og, gts, conv_w, conv_b, bg, hn, init, Ls, "mlstm_s")
            xs = _outffn_call([out_a, hb], wo_parts, xs, gts1, scs2, shs2, gts2, ng, wgu, wdn, Ls, "outffn_ab_s", bb=Bs)
            st_s["a_k"].append(kvn[:, :, :KV_A].reshape(Bs, Ls, HKV_A, HEAD_DIM))
            st_s["a_v"].append(kvn[:, :, KV_A:].reshape(Bs, Ls, HKV_A, HEAD_DIM))
            st_s["b_c"].append(s_c)
            st_s["b_n"].append(s_n)
            st_s["b_m"].append(s_m[:, :, 0])
            st_s["b_conv"].append(conv_tail[:, TAIL_ROWS - (CONV_W - 1):, :])
        else:
            w_in = jnp.concatenate([w_in_c[j][:, :W_C] * QK_SCALE, w_in_c[j][:, W_C:]], axis=1).astype(BF16)
            wo_parts = [w_out_c[j].astype(BF16)]
            bias = _relbias_call(relbias_c[j])
            lc = cache_c_k.shape[2]
            tail_c = NPREV_C * CHUNK

            segs_p = [
                (0, W_C, BF16, False, 0, 1),
                (W_C, 2 * W_C, BF16, True, tail_c, 2),
            ]
            qc, kvc, k_tail, v_tail = _inproj_call(xp, scp1, shp1, g0, w_in, segs_p, tm_p, 1, "inproj_c_p")
            out_c = _band_attn_call(qc, kvc, bias.reshape(H_C * CHUNK, BAND_C), wg_c, groups_c, BAND_C, False,
                                    CHUNKS_PER_STEP_C, PHASE_UNITS_C, "attn_c_p")
            xp = _outffn_call([out_c], wo_parts, xp, gtp1, scp2, shp2, gtp2, ng, wgu, wdn, tm_p, "outffn_c_p")
            st_p["c_k"].append(k_tail.reshape(Bp, tail_c, H_C, HEAD_DIM))
            st_p["c_v"].append(v_tail.reshape(Bp, tail_c, H_C, HEAD_DIM))

            segs_s = [
                (0, W_C, BF16, False, 0, 1),
                (W_C, 2 * W_C, F32, False, 0, 1),
            ]
            qc, kvn = _inproj_call(xs, scs1, shs1, g0, w_in, segs_s, Ls, 0, "inproj_c_s", bb=Bs)
            off = BAND_C - CHUNK - lc
            bias_c = bias[:, :Ls, off:off + lc].reshape(H_C * Ls, lc)
            bias_n = bias[:, :Ls, off + lc:off + lc + Ls].reshape(H_C * Ls, Ls)
            out_c = _cache_attn_call(qc, cache_c_k[j].reshape(Bs, lc, W_C), cache_c_v[j].reshape(Bs, lc, W_C),
                                     kvn, (bias_c, bias_n), wg_c, groups_c, False, "attn_c_s")
            xs = _outffn_call([out_c], wo_parts, xs, gts1, scs2, shs2, gts2, ng, wgu, wdn, Ls, "outffn_c_s", bb=Bs)
            st_s["c_k"].append(kvn[:, :, :W_C].reshape(Bs, Ls, H_C, HEAD_DIM))
            st_s["c_v"].append(kvn[:, :, W_C:].reshape(Bs, Ls, H_C, HEAD_DIM))

    order = ("a_k", "a_v", "b_c", "b_n", "b_m", "b_conv", "c_k", "c_v")
    outs = [xp, xs]
    outs += [jnp.stack(st_p[k]) for k in order]
    outs += [jnp.stack(st_s[k]) for k in order]
    return tuple(outs)
```

```python
import jax
import jax.numpy as jnp
from jax import lax
from jax.experimental import pallas as pl
from jax.experimental.pallas import tpu as pltpu

F32 = jnp.float32
BF16 = jnp.bfloat16

D_MODEL = 1024
CHUNK = 64
HEAD_DIM = 64
HQ_A = 8
HKV_A = 2
WINDOW_A = 128
H_B = 4
DH_B = 128
CONV_W = 4
H_C = 16
NPREV_C = 8
MAX_REL_C = 256
EPS = 1e-6
NEG = -1e30
W_A = HQ_A * HEAD_DIM
KV_A = HKV_A * HEAD_DIM
W_B = H_B * DH_B
W_C = H_C * HEAD_DIM
BAND_A = WINDOW_A + CHUNK
BAND_C = (NPREV_C + 1) * CHUNK
FF_CHUNK = 256
GATE_W = 128
TAIL_ROWS = 8
ROW_TILE = 512
V7X_VMEM_LIMIT = 56 * 1024 * 1024
QK_SCALE = HEAD_DIM ** -0.5
PERM_A = tuple(h for pair in zip(range(HQ_A // HKV_A), range(HQ_A // HKV_A, HQ_A)) for h in pair)
CHUNKS_PER_STEP_A = 8
CHUNKS_PER_STEP_C = 2
PHASE_UNITS_A = 8
PHASE_UNITS_C = 1
MLSTM_SEQS_PER_STEP = 4
MLSTM_CHUNK = 128


def _params(sem, vmem=V7X_VMEM_LIMIT, flags=None):
    return pltpu.CompilerParams(dimension_semantics=sem, vmem_limit_bytes=vmem, flags=flags)


def _const_spec(shape):
    nd = len(shape)
    return pl.BlockSpec(shape, lambda *_: (0,) * nd, pipeline_mode=pl.Buffered(1))


def _split3(x):
    hi = x.astype(BF16)
    r1 = x - hi.astype(F32)
    mid = r1.astype(BF16)
    lo = (r1 - mid.astype(F32)).astype(BF16)
    return hi, mid, lo


def _pad_rows(x, rows):
    if x.shape[0] == rows:
        return x
    return jnp.concatenate([x, jnp.zeros((rows - x.shape[0], x.shape[1]), x.dtype)], axis=0)


def _split3_f32(x):
    return tuple(p.astype(F32) for p in _split3(x))


def _dot(a, b):
    return jnp.dot(a, b, preferred_element_type=F32)


def _dot_nt(a, b):
    return lax.dot_general(a, b, (((1,), (1,)), ((), ())), preferred_element_type=F32)


def _rms(x, g):
    return x * lax.rsqrt(jnp.mean(x * x, axis=-1, keepdims=True) + EPS) * g


def _sigmoid(x):
    return 1.0 / (1.0 + jnp.exp(-x))


def _log_sigmoid(x):
    return jnp.minimum(x, 0.0) - jnp.log(1.0 + jnp.exp(-jnp.abs(x)))


def _ada_kernel(c_ref, w_ref, b_ref, o_ref):
    c = c_ref[...]
    s = c * _sigmoid(c)
    s_hi, s_mid, _ = _split3(s)
    w_hi, w_mid, _ = _split3(w_ref[0])
    acc = _dot(s_hi, w_hi) + _dot(s_hi, w_mid) + _dot(s_mid, w_hi)
    o_ref[0] = acc + b_ref[0]


def _ada_call(c_all, w_ada, b_ada):
    depth, d, n6 = w_ada.shape
    nb = c_all.shape[0]
    tn = 512
    return pl.pallas_call(
        _ada_kernel,
        grid=(depth, n6 // tn),
        in_specs=[
            pl.BlockSpec((nb, d), lambda i, j: (0, 0)),
            pl.BlockSpec((1, d, tn), lambda i, j: (i, 0, j)),
            pl.BlockSpec((1, 1, tn), lambda i, j: (i, 0, j)),
        ],
        out_specs=pl.BlockSpec((1, nb, tn), lambda i, j: (i, 0, j)),
        out_shape=jax.ShapeDtypeStruct((depth, nb, n6), F32),
        compiler_params=_params(("arbitrary", "arbitrary")),
        name="ada_mod",
    )(c_all, w_ada, b_ada.reshape(depth, 1, n6))


def _inproj_call(x, sc, sh, g, w, segs, tm, pad, name, bb=1):
    B, L, d = x.shape
    nT = L // tm
    P = w.shape[1]
    n_seg = len(segs)

    def kernel(x_ref, sc_ref, sh_ref, g_ref, w_ref, *outs):
        seg_refs = outs[:n_seg]
        tail_refs = outs[n_seg:]
        t = pl.program_id(1)

        def compute():
            half = tm // 2 if (bb == 1 and tm % 32 == 0) else tm
            hs = []
            for r0 in range(0, tm, half):
                xs = x_ref[:, r0:r0 + half, :]
                hh = (_rms(xs, g_ref[...]) * (1.0 + sc_ref[...]) + sh_ref[...]).astype(BF16)
                hs.append(hh.reshape(bb * half, d))
            ti = 0
            for (c0, wd, dt, _, tail_rows, tail_split), r in zip(segs, seg_refs):
                ys = [_dot(hh, w_ref[:, c0:c0 + wd]) for hh in hs]
                y = (ys[0] if len(ys) == 1 else jnp.concatenate(ys, axis=0)).reshape(bb, tm, wd)
                r[...] = y.astype(dt)
                if tail_rows:
                    trs = tail_refs[ti:ti + tail_split]
                    ti += tail_split
                    ws = wd // tail_split

                    @pl.when(t == nT - 1 + pad)
                    def _():
                        for k, tr in enumerate(trs):
                            tr[...] = y[:, tm - tail_rows:, k * ws:(k + 1) * ws]

        if pad:
            @pl.when(t == 0)
            def _():
                for (_, wd, dt, padded, _, _), r in zip(segs, seg_refs):
                    if padded:
                        r[...] = jnp.zeros((bb, tm, wd), dt)

            pl.when(t > 0)(compute)
        else:
            compute()

    def row_idx(t):
        return jnp.maximum(t - 1, 0) if pad else t

    in_specs = [
        pl.BlockSpec((bb, tm, d), lambda b, t: (b, row_idx(t), 0)),
        pl.BlockSpec((bb, 1, d), lambda b, t: (b, 0, 0)),
        pl.BlockSpec((bb, 1, d), lambda b, t: (b, 0, 0)),
        _const_spec((1, d)),
        _const_spec((d, P)),
    ]
    out_specs, out_shapes = [], []
    for (_, wd, dt, padded, _, _) in segs:
        if padded:
            out_specs.append(pl.BlockSpec((bb, tm, wd), lambda b, t: (b, t, 0)))
            out_shapes.append(jax.ShapeDtypeStruct((B, tm + L, wd), dt))
        else:
            out_specs.append(pl.BlockSpec((bb, tm, wd), lambda b, t: (b, row_idx(t), 0)))
            out_shapes.append(jax.ShapeDtypeStruct((B, L, wd), dt))
    for (_, wd, _, _, tail_rows, tail_split) in segs:
        for _ in range(tail_split if tail_rows else 0):
            out_specs.append(pl.BlockSpec((bb, tail_rows, wd // tail_split), lambda b, t: (b, 0, 0)))
            out_shapes.append(jax.ShapeDtypeStruct((B, tail_rows, wd // tail_split), F32))

    return pl.pallas_call(
        kernel,
        grid=(B // bb, nT + pad),
        in_specs=in_specs,
        out_specs=out_specs,
        out_shape=out_shapes,
        compiler_params=_params(("arbitrary", "arbitrary")),
        name=name,
    )(x, sc, sh, g, w)


def _attend_grouped(q_tiles, pieces, sink_col, lq):
    return _attend_units([(q_tiles, pieces, sink_col)], lq)[0]


def _attend_units(units, lq):
    wg = units[0][0][0].shape[1]
    nb = wg // HEAD_DIM
    blk = lax.shift_right_logical(lax.broadcasted_iota(jnp.int32, (lq, wg), 1), HEAD_DIM.bit_length() - 1)
    scores = []
    for q_tiles, pieces, _ in units:
        rows = []
        for qt in q_tiles:
            for r in range(nb):
                rows.append(jnp.where(blk == r, qt, 0.0).astype(BF16))
        qbd = jnp.concatenate(rows, axis=0)
        ss = []
        for k, _, bias, valid in pieces:
            s = _dot_nt(qbd, k) + bias
            if valid is not None:
                s = jnp.where(valid, s, NEG)
            ss.append(s)
        scores.append(ss)
    maxima = []
    for ss, (_, _, sink_col) in zip(scores, units):
        m = ss[0].max(axis=-1, keepdims=True)
        for s in ss[1:]:
            m = jnp.maximum(m, s.max(axis=-1, keepdims=True))
        if sink_col is not None:
            m = jnp.maximum(m, sink_col)
        maxima.append(m)
    probs, dens = [], []
    for ss, m, (_, _, sink_col) in zip(scores, maxima, units):
        ps = [jnp.exp(s - m) for s in ss]
        den = ps[0].sum(axis=-1, keepdims=True)
        for p in ps[1:]:
            den = den + p.sum(axis=-1, keepdims=True)
        if sink_col is not None:
            den = den + jnp.exp(sink_col - m)
        probs.append([p.astype(BF16) for p in ps])
        dens.append(den)
    results = []
    for ps, den, (q_tiles, pieces, _) in zip(probs, dens, units):
        o = _dot(ps[0], pieces[0][1])
        for p, piece in zip(ps[1:], pieces[1:]):
            o = o + _dot(p, piece[1])
        o = o / den
        outs = []
        for t in range(len(q_tiles)):
            base = t * nb * lq
            acc = o[base + (nb - 1) * lq:base + nb * lq]
            for r in range(nb - 2, -1, -1):
                acc = jnp.where(blk == r, o[base + r * lq:base + (r + 1) * lq], acc)
            outs.append(acc)
        results.append(outs)
    return results


def _alibi_rows(sinks, lq):
    perm = jnp.array(PERM_A)
    slopes = jnp.exp2(-8.0 * jnp.arange(1, HQ_A + 1, dtype=F32) / HQ_A)
    rc = jnp.stack([-slopes[perm], sinks.astype(F32)[perm]], axis=1)
    return jnp.repeat(rc, lq, axis=0)


def _band_attn_call(q, kv, extra, wg, q_groups, band, alibi, nq, phase_units, name):
    B, L, wq = q.shape
    nsteps = L // (CHUNK * nq)
    hist = band - CHUNK
    kvw = kv.shape[2]
    v_off = kvw // 2
    nb = wg // HEAD_DIM

    def kernel(q_ref, kv_ref, e_ref, o_ref):
        step = pl.program_id(1)
        jj = lax.broadcasted_iota(jnp.int32, (1, band), 1)
        if alibi:
            m_rows = len(q_groups[0][1]) * nb * CHUNK
            ii = lax.broadcasted_iota(jnp.int32, (m_rows, band), 0) & (CHUNK - 1)
            dist = jnp.abs(ii + hist - lax.broadcasted_iota(jnp.int32, (m_rows, band), 1)).astype(F32)
        rows_of = []
        bias_of = []
        for _, q_offs in q_groups:
            r0 = (q_offs[0] // HEAD_DIM) * CHUNK
            r1 = r0 + len(q_offs) * nb * CHUNK
            rows_of.append((r0, r1))
            bias_of.append(e_ref[r0:r1, 0:1] * dist if alibi else None)
        units, dests = [], []
        for cc in range(nq):
            c = step * nq + cc
            start = pl.multiple_of(c * CHUNK + ROW_TILE - hist, CHUNK)
            valid = (c * CHUNK - hist + jj) >= 0
            rs = slice(cc * CHUNK, (cc + 1) * CHUNK)
            for gi, (kv0, q_offs) in enumerate(q_groups):
                kb = kv_ref[0, pl.ds(start, band), kv0:kv0 + wg]
                vb = kv_ref[0, pl.ds(start, band), v_off + kv0:v_off + kv0 + wg]
                tiles = [q_ref[0, rs, o:o + wg].astype(F32) for o in q_offs]
                r0, r1 = rows_of[gi]
                if alibi:
                    units.append((tiles, [(kb, vb, bias_of[gi], valid)], e_ref[r0:r1, 1:2]))
                else:
                    units.append((tiles, [(kb, vb, e_ref[r0:r1, :], valid)], None))
                dests.append((rs, q_offs))
        for u0 in range(0, len(units), phase_units):
            for (rs, q_offs), outs in zip(dests[u0:u0 + phase_units],
                                          _attend_units(units[u0:u0 + phase_units], CHUNK)):
                for o, res in zip(q_offs, outs):
                    o_ref[0, rs, o:o + wg] = res.astype(BF16)

    return pl.pallas_call(
        kernel,
        grid=(B, nsteps),
        in_specs=[
            pl.BlockSpec((1, nq * CHUNK, wq), lambda b, c: (b, c, 0)),
            pl.BlockSpec((1, ROW_TILE + L, kvw), lambda b, c: (b, 0, 0)),
            _const_spec(extra.shape),
        ],
        out_specs=pl.BlockSpec((1, nq * CHUNK, wq), lambda b, c: (b, c, 0)),
        out_shape=jax.ShapeDtypeStruct((B, L, wq), BF16),
        compiler_params=_params(("arbitrary", "arbitrary")),
        name=name,
    )(q, kv, extra)


def _cache_attn_call(q, kc, vc, kvn, extra, wg, q_groups, alibi, name):
    B, T, wq = q.shape
    Lc = kc.shape[1]
    v_off = kvn.shape[2] // 2
    nb = wg // HEAD_DIM
    assert T & (T - 1) == 0

    def kernel(q_ref, kc_ref, vc_ref, kvn_ref, *rest):
        m_rows = len(q_groups[0][1]) * nb * T
        if alibi:
            e_ref, o_ref = rest
            ii = lax.broadcasted_iota(jnp.int32, (m_rows, Lc), 0) & (T - 1)
            dist_c = jnp.abs(ii + Lc - lax.broadcasted_iota(jnp.int32, (m_rows, Lc), 1)).astype(F32)
            ii = lax.broadcasted_iota(jnp.int32, (m_rows, T), 0) & (T - 1)
            dist_n = jnp.abs(ii - lax.broadcasted_iota(jnp.int32, (m_rows, T), 1)).astype(F32)
        else:
            bc_ref, bn_ref, o_ref = rest
        for kv0, q_offs in q_groups:
            k_c = kc_ref[0, :, kv0:kv0 + wg].astype(BF16)
            v_c = vc_ref[0, :, kv0:kv0 + wg].astype(BF16)
            k_n = kvn_ref[0, :, kv0:kv0 + wg].astype(BF16)
            v_n = kvn_ref[0, :, v_off + kv0:v_off + kv0 + wg].astype(BF16)
            tiles = [q_ref[0, :, o:o + wg].astype(F32) for o in q_offs]
            r0 = (q_offs[0] // HEAD_DIM) * T
            r1 = r0 + m_rows
            if alibi:
                slope = e_ref[r0:r1, 0:1]
                outs = _attend_grouped(tiles, [(k_c, v_c, slope * dist_c, None), (k_n, v_n, slope * dist_n, None)],
                                       e_ref[r0:r1, 1:2], T)
            else:
                outs = _attend_grouped(tiles, [(k_c, v_c, bc_ref[r0:r1, :], None),
                                               (k_n, v_n, bn_ref[r0:r1, :], None)], None, T)
            for o, res in zip(q_offs, outs):
                o_ref[0, :, o:o + wg] = res.astype(BF16)

    in_specs = [
        pl.BlockSpec((1, T, wq), lambda b: (b, 0, 0)),
        pl.BlockSpec((1, Lc, kc.shape[2]), lambda b: (b, 0, 0)),
        pl.BlockSpec((1, Lc, vc.shape[2]), lambda b: (b, 0, 0)),
        pl.BlockSpec((1, T, kvn.shape[2]), lambda b: (b, 0, 0)),
    ]
    if alibi:
        in_specs.append(_const_spec(extra.shape))
        args = (q, kc, vc, kvn, extra)
    else:
        in_specs += [_const_spec(extra[0].shape), _const_spec(extra[1].shape)]
        args = (q, kc, vc, kvn, extra[0], extra[1])
    return pl.pallas_call(
        kernel,
        grid=(B,),
        in_specs=in_specs,
        out_specs=pl.BlockSpec((1, T, wq), lambda b: (b, 0, 0)),
        out_shape=jax.ShapeDtypeStruct((B, T, wq), BF16),
        compiler_params=_params(("arbitrary",)),
        name=name,
    )(*args)


def _relbias_kernel(tab_ref, o_ref):
    nt = tab_ref.shape[1]
    hi, mid, lo = _split3(tab_ref[...])
    tt = lax.broadcasted_iota(jnp.int32, (nt, BAND_C), 0)
    jj = lax.broadcasted_iota(jnp.int32, (nt, BAND_C), 1)

    def body(i, carry):
        idx = jnp.clip(i - jj + NPREV_C * CHUNK, -MAX_REL_C, MAX_REL_C) + MAX_REL_C
        oh = jnp.where(tt == idx, 1.0, 0.0).astype(BF16)
        o_ref[i] = _dot(hi, oh) + _dot(mid, oh) + _dot(lo, oh)
        return carry

    lax.fori_loop(0, CHUNK, body, 0)


def _relbias_call(table):
    nh, nt = table.shape
    ntp = ((nt + 127) // 128) * 128
    tab = jnp.pad(table, ((0, 0), (0, ntp - nt)))
    out = pl.pallas_call(
        _relbias_kernel,
        out_shape=jax.ShapeDtypeStruct((CHUNK, nh, BAND_C), F32),
        compiler_params=_params(None),
        name="relbias",
    )(tab)
    return jnp.transpose(out, (1, 0, 2))


def _mlstm_call(qkb, vb, og, gates, conv_w, conv_b, bg, hnorm, init, Lc, name, bb=MLSTM_SEQS_PER_STEP):
    B, L, _ = qkb.shape
    nC = L // Lc
    zero_init = init is None
    k_scale = DH_B ** -0.5

    def kernel(*refs):
        if zero_init:
            (qkb_ref, vb_ref, og_ref, gt_ref, cw_ref, cb_ref, bg_ref, hn_ref,
             hb_ref, C_ref, n_ref, m_ref, tail_ref, ext_ref) = refs
        else:
            (qkb_ref, vb_ref, og_ref, gt_ref, cw_ref, cb_ref, bg_ref, hn_ref,
             C0_ref, n0_ref, m0_ref, conv0_ref,
             hb_ref, C_ref, n_ref, m_ref, tail_ref, ext_ref) = refs
        c = pl.program_id(1)

        @pl.when(c == 0)
        def _():
            if zero_init:
                C_ref[...] = jnp.zeros(C_ref.shape, F32)
                n_ref[...] = jnp.zeros(n_ref.shape, F32)
                m_ref[...] = jnp.zeros(m_ref.shape, F32)
                tail_ref[...] = jnp.zeros(tail_ref.shape, F32)
            else:
                C_ref[...] = C0_ref[...]
                n_ref[...] = n0_ref[...]
                m_ref[...] = m0_ref[...]
                tail_ref[...] = conv0_ref[...]

        r_i = lax.broadcasted_iota(jnp.int32, (Lc, Lc), 0)
        c_i = lax.broadcasted_iota(jnp.int32, (Lc, Lc), 1)
        tri = r_i >= c_i
        tri_bf = jnp.where(tri, 1.0, 0.0).astype(BF16)
        r_u = lax.broadcasted_iota(jnp.int32, (GATE_W, GATE_W), 0)
        c_u = lax.broadcasted_iota(jnp.int32, (GATE_W, GATE_W), 1)
        upper_bf = jnp.where(r_u <= c_u, 1.0, 0.0).astype(BF16)
        cw = cw_ref[...]
        cb = cb_ref[...]
        bgv = bg_ref[...]
        hn = hn_ref[...]

        gbs, g_ts, qks, l_parts, t_parts = [], [], [], [], []
        for b in range(bb):
            u = qkb_ref[b].astype(F32)
            ext_ref[b, 0:TAIL_ROWS, :] = tail_ref[b]
            ext_ref[b, TAIL_ROWS:TAIL_ROWS + Lc, :] = u
            y = cb + cw[CONV_W - 1:CONV_W] * u
            for j in range(CONV_W - 1):
                y = y + cw[j:j + 1] * ext_ref[b, pl.ds(TAIL_ROWS - (CONV_W - 1) + j, Lc), :]
            tail_ref[b] = ext_ref[b, Lc:Lc + TAIL_ROWS, :]
            qks.append(y * _sigmoid(y))

            gb = gt_ref[b] + bgv
            gbs.append(gb)
            l_parts += list(_split3_f32(_log_sigmoid(gb)))
            g_t = _pad_rows(gb, GATE_W).T
            g_ts.append(g_t)
            t_parts += list(_split3_f32(_log_sigmoid(g_t[0:2 * H_B, :])))
        col_sums = _dot(tri_bf, jnp.concatenate(l_parts, axis=1).astype(BF16))
        row_sums = _dot(jnp.concatenate(t_parts, axis=0).astype(BF16), upper_bf)

        def body(b):
            gb = gbs[b]
            qk = qks[b]
            c0 = 3 * b * GATE_W
            b_col = (col_sums[:, c0:c0 + GATE_W] + col_sums[:, c0 + GATE_W:c0 + 2 * GATE_W]
                     + col_sums[:, c0 + 2 * GATE_W:c0 + 3 * GATE_W])
            r0 = 3 * b * 2 * H_B
            b_row = row_sums[r0:r0 + 2 * H_B] + row_sums[r0 + 2 * H_B:r0 + 4 * H_B] + row_sums[r0 + 4 * H_B:r0 + 6 * H_B]
            a_row_all = g_ts[b][0:H_B, :] - b_row[H_B:2 * H_B, :]

            ogv = og_ref[b].astype(F32)
            vv = vb_ref[b]
            for h in range(H_B):
                hs = slice(h * DH_B, (h + 1) * DH_B)
                qh = qk[:, hs]
                kh = qk[:, W_B + h * DH_B:W_B + (h + 1) * DH_B] * k_scale
                vh = vv[:, hs]
                qb = qh.astype(BF16)
                kb = kh.astype(BF16)
                a_row = a_row_all[h:h + 1, 0:Lc]
                bcol = b_col[:, H_B + h:H_B + h + 1]
                a_col = gb[:, h:h + 1] - bcol
                m_prev = m_ref[b, h:h + 1, 0:1]
                amat = jnp.where(tri, a_row, -jnp.inf)
                mc = jnp.maximum(amat.max(axis=-1, keepdims=True), m_prev)
                dm = jnp.exp(amat - mc)
                c_h = C_ref[b, h]
                qck = _dot_nt(qb, jnp.concatenate([c_h.astype(BF16), kb], axis=0))
                w = qck[:, DH_B:DH_B + Lc] * dm
                a_int = jnp.exp(m_prev - mc)
                num = a_int * qck[:, 0:DH_B] + _dot(w.astype(BF16), vh)
                n_h = n_ref[b, h:h + 1, :]
                den = a_int * jnp.sum(qh * n_h, axis=-1, keepdims=True) + w.sum(axis=-1, keepdims=True)
                hh = num / jnp.maximum(jnp.abs(den), jnp.exp(-(bcol + mc)))
                hh = _rms(hh, hn[:, hs]) * _sigmoid(ogv[:, hs])
                hb_ref[b, :, hs] = hh.astype(BF16)

                m_end = mc[Lc - 1:Lc, :]
                a_end = jnp.exp(m_prev - m_end)
                w_end = jnp.exp(a_col - m_end)
                vw = vh.astype(F32) * w_end
                vw_t = _pad_rows(vw, DH_B).T.astype(BF16)
                k_p = _pad_rows(kh, DH_B).astype(BF16)
                C_ref[b, h] = a_end * c_h + _dot(vw_t, k_p)
                n_ref[b, h:h + 1, :] = a_end * n_h + jnp.sum(kh * w_end, axis=0, keepdims=True)
                m_ref[b, h:h + 1, :] = jnp.broadcast_to(bcol[Lc - 1:Lc, :] + m_end, (1, GATE_W))

        for b in range(bb):
            body(b)

    w2 = 2 * W_B
    in_specs = [
        pl.BlockSpec((bb, Lc, w2), lambda i, c: (i, c, 0)),
        pl.BlockSpec((bb, Lc, W_B), lambda i, c: (i, c, 0)),
        pl.BlockSpec((bb, Lc, W_B), lambda i, c: (i, c, 0)),
        pl.BlockSpec((bb, Lc, GATE_W), lambda i, c: (i, c, 0)),
        _const_spec((CONV_W, w2)),
        _const_spec((1, w2)),
        _const_spec((1, GATE_W)),
        _const_spec((1, W_B)),
    ]
    args = [qkb, vb, og, gates, conv_w, conv_b, bg, hnorm]
    if not zero_init:
        in_specs += [
            pl.BlockSpec((bb, H_B, DH_B, DH_B), lambda i, c: (i, 0, 0, 0)),
            pl.BlockSpec((bb, H_B, DH_B), lambda i, c: (i, 0, 0)),
            pl.BlockSpec((bb, H_B, GATE_W), lambda i, c: (i, 0, 0)),
            pl.BlockSpec((bb, TAIL_ROWS, w2), lambda i, c: (i, 0, 0)),
        ]
        args += list(init)
    out_specs = [
        pl.BlockSpec((bb, Lc, W_B), lambda i, c: (i, c, 0)),
        pl.BlockSpec((bb, H_B, DH_B, DH_B), lambda i, c: (i, 0, 0, 0)),
        pl.BlockSpec((bb, H_B, DH_B), lambda i, c: (i, 0, 0)),
        pl.BlockSpec((bb, H_B, GATE_W), lambda i, c: (i, 0, 0)),
    ]
    out_shape = [
        jax.ShapeDtypeStruct((B, L, W_B), BF16),
        jax.ShapeDtypeStruct((B, H_B, DH_B, DH_B), F32),
        jax.ShapeDtypeStruct((B, H_B, DH_B), F32),
        jax.ShapeDtypeStruct((B, H_B, GATE_W), F32),
    ]
    return pl.pallas_call(
        kernel,
        grid=(B // bb, nC),
        in_specs=in_specs,
        out_specs=out_specs,
        out_shape=out_shape,
        scratch_shapes=[pltpu.VMEM((bb, TAIL_ROWS, w2), F32), pltpu.VMEM((bb, TAIL_ROWS + Lc, w2), F32)],
        compiler_params=_params(("arbitrary", "arbitrary")),
        name=name,
    )(*args)


def _outffn_call(parts, wo_parts, x, gt1, sc2, sh2, gt2, ng, ffn_w, tm, name, bb=1):
    B, L, d = x.shape
    nT = L // tm
    n_in = len(parts)
    wg, wu, wd = ffn_w
    fc = FF_CHUNK
    n_ff = wg.shape[1] // fc
    rows = bb * tm

    def kernel(*refs):
        a_refs = refs[:n_in]
        wo_refs = refs[n_in:2 * n_in]
        (x_ref, gt1_ref, sc2_ref, sh2_ref, gt2_ref, ng_ref, wg_ref, wu_ref, wd_ref,
         o_ref, h_ref, z_ref) = refs[2 * n_in:]
        half = tm // 2 if (bb == 1 and tm % 32 == 0) else tm
        for r0 in range(0, tm, half):
            rs = slice(r0, r0 + half)
            y = None
            for a_ref, wo_ref in zip(a_refs, wo_refs):
                part = _dot(a_ref[:, rs, :].reshape(bb * half, a_ref.shape[2]), wo_ref[...])
                y = part if y is None else y + part
            x1 = x_ref[:, rs, :] + gt1_ref[...] * _rms(y, ng_ref[1:2, :]).reshape(bb, half, d)
            o_ref[:, rs, :] = x1
            h = _rms(x1, ng_ref[2:3, :]) * (1.0 + sc2_ref[...]) + sh2_ref[...]
            h_ref[bb * r0:bb * (r0 + half), :] = h.reshape(bb * half, d).astype(BF16)

        for j in range(n_ff):
            cs = slice(j * fc, (j + 1) * fc)
            gpart = _dot(h_ref[...], wg_ref[:, cs])
            upart = _dot(h_ref[...], wu_ref[:, cs])
            z_ref[j] = (gpart * _sigmoid(gpart) * upart).astype(BF16)
        for r0 in range(0, tm, half):
            zr = slice(bb * r0, bb * (r0 + half))
            acc = _dot(z_ref[0, zr, :], wd_ref[0:fc, :])
            for j in range(1, n_ff):
                acc = acc + _dot(z_ref[j, zr, :], wd_ref[j * fc:(j + 1) * fc, :])
            rs = slice(r0, r0 + half)
            o_ref[:, rs, :] = o_ref[:, rs, :] + gt2_ref[...] * _rms(acc, ng_ref[3:4, :]).reshape(bb, half, d)

    mod_spec = pl.BlockSpec((bb, 1, d), lambda b, t: (b, 0, 0))
    in_specs = [pl.BlockSpec((bb, tm, p.shape[2]), lambda b, t: (b, t, 0)) for p in parts]
    in_specs += [_const_spec(w.shape) for w in wo_parts]
    in_specs += [
        pl.BlockSpec((bb, tm, d), lambda b, t: (b, t, 0)),
        mod_spec, mod_spec, mod_spec, mod_spec,
        _const_spec(ng.shape),
        _const_spec(wg.shape),
        _const_spec(wu.shape),
        _const_spec(wd.shape),
    ]
    return pl.pallas_call(
        kernel,
        grid=(B // bb, nT),
        in_specs=in_specs,
        out_specs=pl.BlockSpec((bb, tm, d), lambda b, t: (b, t, 0)),
        out_shape=jax.ShapeDtypeStruct((B, L, d), F32),
        scratch_shapes=[pltpu.VMEM((rows, d), BF16), pltpu.VMEM((n_ff, rows, fc), BF16)],
        compiler_params=_params(("arbitrary", "arbitrary")),
        name=name,
    )(*parts, *wo_parts, x, gt1, sc2, sh2, gt2, ng, wg, wu, wd)


def _perm_heads(w, axis):
    blocks = [lax.slice_in_dim(w, h * HEAD_DIM, (h + 1) * HEAD_DIM, axis=axis) for h in PERM_A]
    return jnp.concatenate(blocks, axis=axis)


def _prep_w_in_ab(w):
    o_k = W_A
    o_qk = W_A + 2 * KV_A
    o_vb = o_qk + 2 * W_B
    o_g = o_vb + W_B
    o_og = o_g + 2 * H_B
    d = w.shape[0]
    cols = [_perm_heads(w[:, :W_A], 1) * QK_SCALE, w[:, o_k:o_qk], w[:, o_qk:o_vb], w[:, o_vb:o_g],
            w[:, o_og:o_og + W_B], w[:, o_g:o_og], jnp.zeros((d, GATE_W - 2 * H_B), w.dtype)]
    return jnp.concatenate(cols, axis=1).astype(BF16)


def _prep_ffn(wg, wu, wd):
    assert wg.shape[1] % FF_CHUNK == 0
    return wg.astype(BF16), wu.astype(BF16), wd.astype(BF16)


def kernel(x_prompt, x_sample, cache_a_k, cache_a_v, state_b_c, state_b_n, state_b_m, state_b_conv, cache_c_k, cache_c_v, c_prompt, c_sample, w_in_ab, sink_a, conv_w_b, conv_b_b, b_gates_b, hnorm_b, w_out_ab, w_in_c, relbias_c, w_out_c, w_ada, b_ada, norm_g, w_ffn_gate, w_ffn_up, w_ffn_down):
    Bp, Lp, d = x_prompt.shape
    Bs, Ls, _ = x_sample.shape
    depth = w_ada.shape[0]
    tm_p = ROW_TILE
    assert Lp % ROW_TILE == 0 and Lp % (CHUNK * CHUNKS_PER_STEP_A) == 0 and Lp % (CHUNK * CHUNKS_PER_STEP_C) == 0
    assert (Bp + Bs) % 8 == 0 and Ls % 8 == 0 and Ls >= CONV_W - 1 and Bp % MLSTM_SEQS_PER_STEP == 0 and Bs % MLSTM_SEQS_PER_STEP == 0

    mods = _ada_call(jnp.concatenate([c_prompt, c_sample], axis=0), w_ada, b_ada)

    def mod_parts(i, lo, hi):
        m = mods[i, lo:hi].reshape(hi - lo, 1, 6, d)
        return [m[:, :, k, :] for k in range(6)]

    c_qa, c_kv, c_qkb = 0, W_A, W_A + 2 * KV_A
    c_vb = c_qkb + 2 * W_B
    c_og = c_vb + W_B
    c_gt = c_og + W_B
    groups_a = [(0, list(range(0, W_A, KV_A)))]
    wg_c = 4 * HEAD_DIM
    groups_c = [(o, [o]) for o in range(0, W_C, wg_c)]

    xp, xs = x_prompt, x_sample
    st_p = {k: [] for k in ("a_k", "a_v", "b_c", "b_n", "b_m", "b_conv", "c_k", "c_v")}
    st_s = {k: [] for k in st_p}

    for i in range(depth):
        j = i // 2
        ng = norm_g[i]
        g0 = ng[0:1]
        shp1, scp1, gtp1, shp2, scp2, gtp2 = mod_parts(i, 0, Bp)
        shs1, scs1, gts1, shs2, scs2, gts2 = mod_parts(i, Bp, Bp + Bs)
        ffn_w = _prep_ffn(w_ffn_gate[i], w_ffn_up[i], w_ffn_down[i])

        if i % 2 == 0:
            w_in = _prep_w_in_ab(w_in_ab[j])
            wo_parts = [_perm_heads(w_out_ab[j][:W_A], 0).astype(BF16), w_out_ab[j][W_A:].astype(BF16)]
            conv_w = conv_w_b[j]
            conv_b = conv_b_b[j].reshape(1, 2 * W_B)
            bg = jnp.pad(b_gates_b[j], (0, GATE_W - 2 * H_B)).reshape(1, GATE_W)
            hn = hnorm_b[j].reshape(1, W_B)

            segs_p = [
                (c_qa, W_A, BF16, False, 0, 1),
                (c_kv, 2 * KV_A, BF16, True, WINDOW_A, 2),
                (c_qkb, 2 * W_B, BF16, False, TAIL_ROWS, 1),
                (c_vb, W_B, BF16, False, 0, 1),
                (c_og, W_B, BF16, False, 0, 1),
                (c_gt, GATE_W, F32, False, 0, 1),
            ]
            qa, kva, qkb, vb, og, gts, k_tail, v_tail, conv_tail = _inproj_call(
                xp, scp1, shp1, g0, w_in, segs_p, tm_p, 1, "inproj_ab_p")
            out_a = _band_attn_call(qa, kva, _alibi_rows(sink_a[j], CHUNK), KV_A, groups_a, BAND_A, True,
                                    CHUNKS_PER_STEP_A, PHASE_UNITS_A, "attn_a_p")
            hb, s_c, s_n, s_m = _mlstm_call(qkb, vb, og, gts, conv_w, conv_b, bg, hn, None, MLSTM_CHUNK, "mlstm_p")
            xp = _outffn_call([out_a, hb], wo_parts, xp, gtp1, scp2, shp2, gtp2, ng, ffn_w, tm_p, "outffn_ab_p")
            st_p["a_k"].append(k_tail.reshape(Bp, WINDOW_A, HKV_A, HEAD_DIM))
            st_p["a_v"].append(v_tail.reshape(Bp, WINDOW_A, HKV_A, HEAD_DIM))
            st_p["b_c"].append(s_c)
            st_p["b_n"].append(s_n)
            st_p["b_m"].append(s_m[:, :, 0])
            st_p["b_conv"].append(conv_tail[:, TAIL_ROWS - (CONV_W - 1):, :])

            segs_s = [
                (c_qa, W_A, BF16, False, 0, 1),
                (c_kv, 2 * KV_A, F32, False, 0, 1),
                (c_qkb, 2 * W_B, BF16, False, TAIL_ROWS, 1),
                (c_vb, W_B, BF16, False, 0, 1),
                (c_og, W_B, BF16, False, 0, 1),
                (c_gt, GATE_W, F32, False, 0, 1),
            ]
            qa, kvn, qkb, vb, og, gts, conv_tail = _inproj_call(
                xs, scs1, shs1, g0, w_in, segs_s, Ls, 0, "inproj_ab_s", bb=Bs)
            la = cache_a_k.shape[2]
            out_a = _cache_attn_call(qa, cache_a_k[j].reshape(Bs, la, KV_A), cache_a_v[j].reshape(Bs, la, KV_A),
                                     kvn, _alibi_rows(sink_a[j], Ls), KV_A, groups_a, True, "attn_a_s")
            init = (state_b_c[j], state_b_n[j],
                    jnp.broadcast_to(state_b_m[j][:, :, None], (Bs, H_B, GATE_W)),
                    jnp.pad(state_b_conv[j], ((0, 0), (TAIL_ROWS - (CONV_W - 1), 0), (0, 0))))
            hb, s_c, s_n, s_m = _mlstm_call(qkb, vb, og, gts, conv_w, conv_b, bg, hn, init, Ls, "mlstm_s")
            xs = _outffn_call([out_a, hb], wo_parts, xs, gts1, scs2, shs2, gts2, ng, ffn_w, Ls, "outffn_ab_s", bb=Bs)
            st_s["a_k"].append(kvn[:, :, :KV_A].reshape(Bs, Ls, HKV_A, HEAD_DIM))
            st_s["a_v"].append(kvn[:, :, KV_A:].reshape(Bs, Ls, HKV_A, HEAD_DIM))
            st_s["b_c"].append(s_c)
            st_s["b_n"].append(s_n)
            st_s["b_m"].append(s_m[:, :, 0])
            st_s["b_conv"].append(conv_tail[:, TAIL_ROWS - (CONV_W - 1):, :])
        else:
            col_scale = jnp.where(jnp.arange(3 * W_C) < W_C, QK_SCALE, 1.0).astype(F32)
            w_in = (w_in_c[j] * col_scale[None, :]).astype(BF16)
            wo_parts = [w_out_c[j].astype(BF16)]
            bias = _relbias_call(relbias_c[j])
            lc = cache_c_k.shape[2]
            tail_c = NPREV_C * CHUNK

            segs_p = [
                (0, W_C, BF16, False, 0, 1),
                (W_C, 2 * W_C, BF16, True, tail_c, 2),
            ]
            qc, kvc, k_tail, v_tail = _inproj_call(xp, scp1, shp1, g0, w_in, segs_p, tm_p, 1, "inproj_c_p")
            out_c = _band_attn_call(qc, kvc, bias.reshape(H_C * CHUNK, BAND_C), wg_c, groups_c, BAND_C, False,
                                    CHUNKS_PER_STEP_C, PHASE_UNITS_C, "attn_c_p")
            xp = _outffn_call([out_c], wo_parts, xp, gtp1, scp2, shp2, gtp2, ng, ffn_w, tm_p, "outffn_c_p")
            st_p["c_k"].append(k_tail.reshape(Bp, tail_c, H_C, HEAD_DIM))
            st_p["c_v"].append(v_tail.reshape(Bp, tail_c, H_C, HEAD_DIM))

            segs_s = [
                (0, W_C, BF16, False, 0, 1),
                (W_C, 2 * W_C, F32, False, 0, 1),
            ]
            qc, kvn = _inproj_call(xs, scs1, shs1, g0, w_in, segs_s, Ls, 0, "inproj_c_s", bb=Bs)
            off = BAND_C - CHUNK - lc
            bias_c = bias[:, :Ls, off:off + lc].reshape(H_C * Ls, lc)
            bias_n = bias[:, :Ls, off + lc:off + lc + Ls].reshape(H_C * Ls, Ls)
            out_c = _cache_attn_call(qc, cache_c_k[j].reshape(Bs, lc, W_C), cache_c_v[j].reshape(Bs, lc, W_C),
                                     kvn, (bias_c, bias_n), wg_c, groups_c, False, "attn_c_s")
            xs = _outffn_call([out_c], wo_parts, xs, gts1, scs2, shs2, gts2, ng, ffn_w, Ls, "outffn_c_s", bb=Bs)
            st_s["c_k"].append(kvn[:, :, :W_C].reshape(Bs, Ls, H_C, HEAD_DIM))
            st_s["c_v"].append(kvn[:, :, W_C:].reshape(Bs, Ls, H_C, HEAD_DIM))

    order = ("a_k", "a_v", "b_c", "b_n", "b_m", "b_conv", "c_k", "c_v")
    outs = [xp, xs]
    outs += [jnp.stack(st_p[k]) for k in order]
    outs += [jnp.stack(st_s[k]) for k in order]
    return tuple(outs)
```

```python
import jax
import jax.numpy as jnp
from jax import lax
from jax.experimental import pallas as pl
from jax.experimental.pallas import tpu as pltpu

F32 = jnp.float32
BF16 = jnp.bfloat16

D_MODEL = 1024
CHUNK = 64
HEAD_DIM = 64
HQ_A = 8
HKV_A = 2
WINDOW_A = 128
H_B = 4
DH_B = 128
CONV_W = 4
H_C = 16
NPREV_C = 8
MAX_REL_C = 256
EPS = 1e-6
NEG = -1e30
W_A = HQ_A * HEAD_DIM
KV_A = HKV_A * HEAD_DIM
W_B = H_B * DH_B
W_C = H_C * HEAD_DIM
BAND_A = WINDOW_A + CHUNK
BAND_C = (NPREV_C + 1) * CHUNK
FF_CHUNK = 256
GATE_W = 128
TAIL_ROWS = 8
ROW_TILE = 512
V7X_VMEM_LIMIT = 56 * 1024 * 1024
LOG2E = 1.4426950408889634
QK_SCALE = HEAD_DIM ** -0.5 * LOG2E
PERM_A = tuple(h for pair in zip(range(HQ_A // HKV_A), range(HQ_A // HKV_A, HQ_A)) for h in pair)
CHUNKS_PER_STEP_A = 8
CHUNKS_PER_STEP_C = 2
PHASE_UNITS_A = 8
PHASE_UNITS_C = 1
MLSTM_SEQS_PER_STEP = 4
MLSTM_CHUNK = 128


def _params(sem, vmem=V7X_VMEM_LIMIT, flags=None):
    return pltpu.CompilerParams(dimension_semantics=sem, vmem_limit_bytes=vmem, flags=flags)


def _const_spec(shape):
    nd = len(shape)
    return pl.BlockSpec(shape, lambda *_: (0,) * nd, pipeline_mode=pl.Buffered(1))


def _split3(x):
    hi = x.astype(BF16)
    r1 = x - hi.astype(F32)
    mid = r1.astype(BF16)
    lo = (r1 - mid.astype(F32)).astype(BF16)
    return hi, mid, lo


def _pad_rows(x, rows):
    if x.shape[0] == rows:
        return x
    return jnp.concatenate([x, jnp.zeros((rows - x.shape[0], x.shape[1]), x.dtype)], axis=0)


def _split3_f32(x):
    return tuple(p.astype(F32) for p in _split3(x))


def _dot(a, b):
    return jnp.dot(a, b, preferred_element_type=F32)


def _dot_nt(a, b):
    return lax.dot_general(a, b, (((1,), (1,)), ((), ())), preferred_element_type=F32)


def _rms(x, g):
    return x * lax.rsqrt(jnp.mean(x * x, axis=-1, keepdims=True) + EPS) * g


def _sigmoid(x):
    return 1.0 / (1.0 + jnp.exp(-x))


def _log_sigmoid(x):
    return jnp.minimum(x, 0.0) - jnp.log(1.0 + jnp.exp(-jnp.abs(x)))


def _ada_kernel(c_ref, w_ref, b_ref, o_ref):
    c = c_ref[...]
    s = c * _sigmoid(c)
    s_hi, s_mid, _ = _split3(s)
    w_hi, w_mid, _ = _split3(w_ref[0])
    acc = _dot(s_hi, w_hi) + _dot(s_hi, w_mid) + _dot(s_mid, w_hi)
    o_ref[0] = acc + b_ref[0]


def _ada_call(c_all, w_ada, b_ada):
    depth, d, n6 = w_ada.shape
    nb = c_all.shape[0]
    tn = 512
    return pl.pallas_call(
        _ada_kernel,
        grid=(depth, n6 // tn),
        in_specs=[
            pl.BlockSpec((nb, d), lambda i, j: (0, 0)),
            pl.BlockSpec((1, d, tn), lambda i, j: (i, 0, j)),
            pl.BlockSpec((1, 1, tn), lambda i, j: (i, 0, j)),
        ],
        out_specs=pl.BlockSpec((1, nb, tn), lambda i, j: (i, 0, j)),
        out_shape=jax.ShapeDtypeStruct((depth, nb, n6), F32),
        compiler_params=_params(("arbitrary", "arbitrary")),
        name="ada_mod",
    )(c_all, w_ada, b_ada.reshape(depth, 1, n6))


def _inproj_call(x, sc, sh, g, w, segs, tm, pad, name, bb=1):
    B, L, d = x.shape
    nT = L // tm
    P = w.shape[1]
    n_seg = len(segs)

    def kernel(x_ref, sc_ref, sh_ref, g_ref, w_ref, *outs):
        seg_refs = outs[:n_seg]
        tail_refs = outs[n_seg:]
        t = pl.program_id(1)

        def compute():
            half = tm // 2 if (bb == 1 and tm % 32 == 0) else tm
            hs = []
            for r0 in range(0, tm, half):
                xs = x_ref[:, r0:r0 + half, :]
                hh = (_rms(xs, g_ref[...]) * (1.0 + sc_ref[...]) + sh_ref[...]).astype(BF16)
                hs.append(hh.reshape(bb * half, d))
            ti = 0
            for (c0, wd, dt, _, tail_rows, tail_split), r in zip(segs, seg_refs):
                ys = [_dot(hh, w_ref[:, c0:c0 + wd]) for hh in hs]
                y = (ys[0] if len(ys) == 1 else jnp.concatenate(ys, axis=0)).reshape(bb, tm, wd)
                r[...] = y.astype(dt)
                if tail_rows:
                    trs = tail_refs[ti:ti + tail_split]
                    ti += tail_split
                    ws = wd // tail_split

                    @pl.when(t == nT - 1 + pad)
                    def _():
                        for k, tr in enumerate(trs):
                            tr[...] = y[:, tm - tail_rows:, k * ws:(k + 1) * ws]

        if pad:
            @pl.when(t == 0)
            def _():
                for (_, wd, dt, padded, _, _), r in zip(segs, seg_refs):
                    if padded:
                        r[...] = jnp.zeros((bb, tm, wd), dt)

            pl.when(t > 0)(compute)
        else:
            compute()

    def row_idx(t):
        return jnp.maximum(t - 1, 0) if pad else t

    in_specs = [
        pl.BlockSpec((bb, tm, d), lambda b, t: (b, row_idx(t), 0)),
        pl.BlockSpec((bb, 1, d), lambda b, t: (b, 0, 0)),
        pl.BlockSpec((bb, 1, d), lambda b, t: (b, 0, 0)),
        _const_spec((1, d)),
        _const_spec((d, P)),
    ]
    out_specs, out_shapes = [], []
    for (_, wd, dt, padded, _, _) in segs:
        if padded:
            out_specs.append(pl.BlockSpec((bb, tm, wd), lambda b, t: (b, t, 0)))
            out_shapes.append(jax.ShapeDtypeStruct((B, tm + L, wd), dt))
        else:
            out_specs.append(pl.BlockSpec((bb, tm, wd), lambda b, t: (b, row_idx(t), 0)))
            out_shapes.append(jax.ShapeDtypeStruct((B, L, wd), dt))
    for (_, wd, _, _, tail_rows, tail_split) in segs:
        for _ in range(tail_split if tail_rows else 0):
            out_specs.append(pl.BlockSpec((bb, tail_rows, wd // tail_split), lambda b, t: (b, 0, 0)))
            out_shapes.append(jax.ShapeDtypeStruct((B, tail_rows, wd // tail_split), F32))

    return pl.pallas_call(
        kernel,
        grid=(B // bb, nT + pad),
        in_specs=in_specs,
        out_specs=out_specs,
        out_shape=out_shapes,
        compiler_params=_params(("arbitrary", "arbitrary")),
        name=name,
    )(x, sc, sh, g, w)


def _attend_grouped(q_tiles, pieces, sink_col, lq):
    return _attend_units([(q_tiles, pieces, sink_col)], lq)[0]


def _attend_units(units, lq):
    wg = units[0][0][0].shape[1]
    nb = wg // HEAD_DIM
    blk = lax.shift_right_logical(lax.broadcasted_iota(jnp.int32, (lq, wg), 1), HEAD_DIM.bit_length() - 1)
    scores = []
    for q_tiles, pieces, _ in units:
        rows = []
        for qt in q_tiles:
            for r in range(nb):
                rows.append(jnp.where(blk == r, qt, 0.0).astype(BF16))
        qbd = jnp.concatenate(rows, axis=0)
        ss = []
        for k, _, bias, valid in pieces:
            s = _dot_nt(qbd, k) + bias
            if valid is not None:
                s = jnp.where(valid, s, NEG)
            ss.append(s)
        scores.append(ss)
    maxima = []
    for ss, (_, _, sink_col) in zip(scores, units):
        m = ss[0].max(axis=-1, keepdims=True)
        for s in ss[1:]:
            m = jnp.maximum(m, s.max(axis=-1, keepdims=True))
        if sink_col is not None:
            m = jnp.maximum(m, sink_col)
        maxima.append(m)
    probs, dens = [], []
    for ss, m, (_, _, sink_col) in zip(scores, maxima, units):
        ps = [jnp.exp2(s - m) for s in ss]
        den = ps[0].sum(axis=-1, keepdims=True)
        for p in ps[1:]:
            den = den + p.sum(axis=-1, keepdims=True)
        if sink_col is not None:
            den = den + jnp.exp2(sink_col - m)
        probs.append([p.astype(BF16) for p in ps])
        dens.append(den)
    results = []
    for ps, den, (q_tiles, pieces, _) in zip(probs, dens, units):
        o = _dot(ps[0], pieces[0][1])
        for p, piece in zip(ps[1:], pieces[1:]):
            o = o + _dot(p, piece[1])
        o = o / den
        outs = []
        for t in range(len(q_tiles)):
            base = t * nb * lq
            acc = o[base + (nb - 1) * lq:base + nb * lq]
            for r in range(nb - 2, -1, -1):
                acc = jnp.where(blk == r, o[base + r * lq:base + (r + 1) * lq], acc)
            outs.append(acc)
        results.append(outs)
    return results


def _alibi_rows(sinks, lq):
    perm = jnp.array(PERM_A)
    slopes = jnp.exp2(-8.0 * jnp.arange(1, HQ_A + 1, dtype=F32) / HQ_A)
    rc = jnp.stack([-slopes[perm], sinks.astype(F32)[perm]], axis=1) * LOG2E
    return jnp.repeat(rc, lq, axis=0)


def _band_attn_a_call(q, kv, rowc, nq, name):
    B, L, wq = q.shape
    nsteps = L // (CHUNK * nq)
    band, hist = BAND_A, WINDOW_A
    kp = 2 * KV_A
    m_lanes = HQ_A * CHUNK
    n_tiles = W_A // KV_A

    def kernel(q_ref, kv_ref, e_ref, o_ref):
        step = pl.program_id(1)
        blk = lax.shift_right_logical(lax.broadcasted_iota(jnp.int32, (CHUNK, KV_A), 1), HEAD_DIM.bit_length() - 1)
        jj = lax.broadcasted_iota(jnp.int32, (band, m_lanes), 0)
        ii = lax.broadcasted_iota(jnp.int32, (band, m_lanes), 1) & (CHUNK - 1)
        bias = e_ref[0:1, :] * jnp.abs(ii + hist - jj).astype(F32)
        sink = e_ref[1:2, :]
        zeros_v = jnp.zeros((kp - band, KV_A), F32)
        zeros_p = jnp.zeros((kp - band, m_lanes), F32)

        def run(masked):
            scores, vts = [], []
            for cc in range(nq):
                c = step * nq + cc
                start = pl.multiple_of(c * CHUNK + ROW_TILE - hist, CHUNK)
                rs = slice(cc * CHUNK, (cc + 1) * CHUNK)
                rows = []
                for t in range(n_tiles):
                    qt = q_ref[0, rs, t * KV_A:(t + 1) * KV_A].astype(F32)
                    for r in range(HKV_A):
                        rows.append(jnp.where(blk == r, qt, 0.0).astype(BF16))
                qbd = jnp.concatenate(rows, axis=0)
                s = _dot_nt(kv_ref[0, pl.ds(start, band), 0:KV_A], qbd) + bias
                if masked:
                    s = jnp.where((c * CHUNK - hist + jj) >= 0, s, NEG)
                scores.append(s)
                v = kv_ref[0, pl.ds(start, band), KV_A:2 * KV_A].astype(F32)
                vts.append(jnp.concatenate([v, zeros_v], axis=0).T.astype(BF16))
            maxima = [jnp.maximum(s.max(axis=0, keepdims=True), sink) for s in scores]
            probs = []
            for s, m in zip(scores, maxima):
                p = jnp.exp2(s - m)
                den = p.sum(axis=0, keepdims=True) + jnp.exp2(sink - m)
                probs.append(jnp.concatenate([p / den, zeros_p], axis=0).astype(BF16))
            for cc, (vt, pn) in enumerate(zip(vts, probs)):
                o = _dot(vt, pn).T
                rs = slice(cc * CHUNK, (cc + 1) * CHUNK)
                for t in range(n_tiles):
                    base = t * HKV_A * CHUNK
                    acc = o[base + (HKV_A - 1) * CHUNK:base + HKV_A * CHUNK]
                    for r in range(HKV_A - 2, -1, -1):
                        acc = jnp.where(blk == r, o[base + r * CHUNK:base + (r + 1) * CHUNK], acc)
                    o_ref[0, rs, t * KV_A:(t + 1) * KV_A] = acc.astype(BF16)

        reaches_pad = step * (nq * CHUNK) < hist
        pl.when(reaches_pad)(lambda: run(True))
        pl.when(jnp.logical_not(reaches_pad))(lambda: run(False))

    return pl.pallas_call(
        kernel,
        grid=(B, nsteps),
        in_specs=[
            pl.BlockSpec((1, nq * CHUNK, wq), lambda b, c: (b, c, 0)),
            pl.BlockSpec((1, ROW_TILE + L, kv.shape[2]), lambda b, c: (b, 0, 0)),
            _const_spec(rowc.shape),
        ],
        out_specs=pl.BlockSpec((1, nq * CHUNK, wq), lambda b, c: (b, c, 0)),
        out_shape=jax.ShapeDtypeStruct((B, L, wq), BF16),
        compiler_params=_params(("arbitrary", "arbitrary")),
        name=name,
    )(q, kv, rowc)


def _band_attn_call(q, kv, extra, wg, q_groups, band, alibi, nq, phase_units, name):
    B, L, wq = q.shape
    nsteps = L // (CHUNK * nq)
    hist = band - CHUNK
    kvw = kv.shape[2]
    v_off = kvw // 2
    nb = wg // HEAD_DIM

    def kernel(q_ref, kv_ref, e_ref, o_ref):
        step = pl.program_id(1)
        jj = lax.broadcasted_iota(jnp.int32, (1, band), 1)
        if alibi:
            m_rows = len(q_groups[0][1]) * nb * CHUNK
            ii = lax.broadcasted_iota(jnp.int32, (m_rows, band), 0) & (CHUNK - 1)
            dist = jnp.abs(ii + hist - lax.broadcasted_iota(jnp.int32, (m_rows, band), 1)).astype(F32)
        rows_of = []
        bias_of = []
        for _, q_offs in q_groups:
            r0 = (q_offs[0] // HEAD_DIM) * CHUNK
            r1 = r0 + len(q_offs) * nb * CHUNK
            rows_of.append((r0, r1))
            bias_of.append(e_ref[r0:r1, 0:1] * dist if alibi else None)
        def run(masked):
            units, dests = [], []
            for cc in range(nq):
                c = step * nq + cc
                start = pl.multiple_of(c * CHUNK + ROW_TILE - hist, CHUNK)
                valid = ((c * CHUNK - hist + jj) >= 0) if masked else None
                rs = slice(cc * CHUNK, (cc + 1) * CHUNK)
                for gi, (kv0, q_offs) in enumerate(q_groups):
                    kb = kv_ref[0, pl.ds(start, band), kv0:kv0 + wg]
                    vb = kv_ref[0, pl.ds(start, band), v_off + kv0:v_off + kv0 + wg]
                    tiles = [q_ref[0, rs, o:o + wg].astype(F32) for o in q_offs]
                    r0, r1 = rows_of[gi]
                    if alibi:
                        units.append((tiles, [(kb, vb, bias_of[gi], valid)], e_ref[r0:r1, 1:2]))
                    else:
                        units.append((tiles, [(kb, vb, e_ref[r0:r1, :], valid)], None))
                    dests.append((rs, q_offs))
            for u0 in range(0, len(units), phase_units):
                for (rs, q_offs), outs in zip(dests[u0:u0 + phase_units],
                                              _attend_units(units[u0:u0 + phase_units], CHUNK)):
                    for o, res in zip(q_offs, outs):
                        o_ref[0, rs, o:o + wg] = res.astype(BF16)

        reaches_pad = step * (nq * CHUNK) < hist
        pl.when(reaches_pad)(lambda: run(True))
        pl.when(jnp.logical_not(reaches_pad))(lambda: run(False))

    return pl.pallas_call(
        kernel,
        grid=(B, nsteps),
        in_specs=[
            pl.BlockSpec((1, nq * CHUNK, wq), lambda b, c: (b, c, 0)),
            pl.BlockSpec((1, ROW_TILE + L, kvw), lambda b, c: (b, 0, 0)),
            _const_spec(extra.shape),
        ],
        out_specs=pl.BlockSpec((1, nq * CHUNK, wq), lambda b, c: (b, c, 0)),
        out_shape=jax.ShapeDtypeStruct((B, L, wq), BF16),
        compiler_params=_params(("arbitrary", "arbitrary")),
        name=name,
    )(q, kv, extra)


def _cache_attn_call(q, kc, vc, kvn, extra, wg, q_groups, alibi, name):
    B, T, wq = q.shape
    Lc = kc.shape[1]
    v_off = kvn.shape[2] // 2
    nb = wg // HEAD_DIM
    assert T & (T - 1) == 0

    def kernel(q_ref, kc_ref, vc_ref, kvn_ref, *rest):
        m_rows = len(q_groups[0][1]) * nb * T
        if alibi:
            e_ref, o_ref = rest
            ii = lax.broadcasted_iota(jnp.int32, (m_rows, Lc), 0) & (T - 1)
            dist_c = jnp.abs(ii + Lc - lax.broadcasted_iota(jnp.int32, (m_rows, Lc), 1)).astype(F32)
            ii = lax.broadcasted_iota(jnp.int32, (m_rows, T), 0) & (T - 1)
            dist_n = jnp.abs(ii - lax.broadcasted_iota(jnp.int32, (m_rows, T), 1)).astype(F32)
        else:
            bc_ref, bn_ref, o_ref = rest
        for kv0, q_offs in q_groups:
            k_c = kc_ref[0, :, kv0:kv0 + wg].astype(BF16)
            v_c = vc_ref[0, :, kv0:kv0 + wg].astype(BF16)
            k_n = kvn_ref[0, :, kv0:kv0 + wg].astype(BF16)
            v_n = kvn_ref[0, :, v_off + kv0:v_off + kv0 + wg].astype(BF16)
            tiles = [q_ref[0, :, o:o + wg].astype(F32) for o in q_offs]
            r0 = (q_offs[0] // HEAD_DIM) * T
            r1 = r0 + m_rows
            if alibi:
                slope = e_ref[r0:r1, 0:1]
                outs = _attend_grouped(tiles, [(k_c, v_c, slope * dist_c, None), (k_n, v_n, slope * dist_n, None)],
                                       e_ref[r0:r1, 1:2], T)
            else:
                outs = _attend_grouped(tiles, [(k_c, v_c, bc_ref[r0:r1, :], None),
                                               (k_n, v_n, bn_ref[r0:r1, :], None)], None, T)
            for o, res in zip(q_offs, outs):
                o_ref[0, :, o:o + wg] = res.astype(BF16)

    in_specs = [
        pl.BlockSpec((1, T, wq), lambda b: (b, 0, 0)),
        pl.BlockSpec((1, Lc, kc.shape[2]), lambda b: (b, 0, 0)),
        pl.BlockSpec((1, Lc, vc.shape[2]), lambda b: (b, 0, 0)),
        pl.BlockSpec((1, T, kvn.shape[2]), lambda b: (b, 0, 0)),
    ]
    if alibi:
        in_specs.append(_const_spec(extra.shape))
        args = (q, kc, vc, kvn, extra)
    else:
        in_specs += [_const_spec(extra[0].shape), _const_spec(extra[1].shape)]
        args = (q, kc, vc, kvn, extra[0], extra[1])
    return pl.pallas_call(
        kernel,
        grid=(B,),
        in_specs=in_specs,
        out_specs=pl.BlockSpec((1, T, wq), lambda b: (b, 0, 0)),
        out_shape=jax.ShapeDtypeStruct((B, T, wq), BF16),
        compiler_params=_params(("arbitrary",)),
        name=name,
    )(*args)


def _relbias_kernel(tab_ref, o_ref):
    nt = tab_ref.shape[1]
    hi, mid, lo = _split3(tab_ref[...])
    tt = lax.broadcasted_iota(jnp.int32, (nt, BAND_C), 0)
    jj = lax.broadcasted_iota(jnp.int32, (nt, BAND_C), 1)

    def body(i, carry):
        idx = jnp.clip(i - jj + NPREV_C * CHUNK, -MAX_REL_C, MAX_REL_C) + MAX_REL_C
        oh = jnp.where(tt == idx, 1.0, 0.0).astype(BF16)
        o_ref[i] = (_dot(hi, oh) + _dot(mid, oh) + _dot(lo, oh)) * LOG2E
        return carry

    lax.fori_loop(0, CHUNK, body, 0)


def _relbias_call(table):
    nh, nt = table.shape
    ntp = ((nt + 127) // 128) * 128
    tab = jnp.pad(table, ((0, 0), (0, ntp - nt)))
    out = pl.pallas_call(
        _relbias_kernel,
        out_shape=jax.ShapeDtypeStruct((CHUNK, nh, BAND_C), F32),
        compiler_params=_params(None),
        name="relbias",
    )(tab)
    return jnp.transpose(out, (1, 0, 2))


def _mlstm_call(qkb, vb, og, gates, conv_w, conv_b, bg, hnorm, init, Lc, name, bb=MLSTM_SEQS_PER_STEP):
    B, L, _ = qkb.shape
    nC = L // Lc
    zero_init = init is None
    k_scale = DH_B ** -0.5

    def kernel(*refs):
        if zero_init:
            (qkb_ref, vb_ref, og_ref, gt_ref, cw_ref, cb_ref, bg_ref, hn_ref,
             hb_ref, C_ref, n_ref, m_ref, tail_ref, ext_ref) = refs
        else:
            (qkb_ref, vb_ref, og_ref, gt_ref, cw_ref, cb_ref, bg_ref, hn_ref,
             C0_ref, n0_ref, m0_ref, conv0_ref,
             hb_ref, C_ref, n_ref, m_ref, tail_ref, ext_ref) = refs
        c = pl.program_id(1)

        @pl.when(c == 0)
        def _():
            if zero_init:
                C_ref[...] = jnp.zeros(C_ref.shape, F32)
                n_ref[...] = jnp.zeros(n_ref.shape, F32)
                m_ref[...] = jnp.zeros(m_ref.shape, F32)
                tail_ref[...] = jnp.zeros(tail_ref.shape, F32)
            else:
                C_ref[...] = C0_ref[...]
                n_ref[...] = n0_ref[...]
                m_ref[...] = m0_ref[...]
                tail_ref[...] = conv0_ref[...]

        r_i = lax.broadcasted_iota(jnp.int32, (Lc, Lc), 0)
        c_i = lax.broadcasted_iota(jnp.int32, (Lc, Lc), 1)
        tri = r_i >= c_i
        tri_bf = jnp.where(tri, 1.0, 0.0).astype(BF16)
        r_u = lax.broadcasted_iota(jnp.int32, (GATE_W, GATE_W), 0)
        c_u = lax.broadcasted_iota(jnp.int32, (GATE_W, GATE_W), 1)
        upper_bf = jnp.where(r_u <= c_u, 1.0, 0.0).astype(BF16)
        cw = cw_ref[...]
        cb = cb_ref[...]
        bgv = bg_ref[...]
        hn = hn_ref[...]

        gbs, g_ts, qks, l_parts, t_parts = [], [], [], [], []
        for b in range(bb):
            u = qkb_ref[b].astype(F32)
            ext_ref[b, 0:TAIL_ROWS, :] = tail_ref[b]
            ext_ref[b, TAIL_ROWS:TAIL_ROWS + Lc, :] = u
            y = cb + cw[CONV_W - 1:CONV_W] * u
            for j in range(CONV_W - 1):
                y = y + cw[j:j + 1] * ext_ref[b, pl.ds(TAIL_ROWS - (CONV_W - 1) + j, Lc), :]
            tail_ref[b] = ext_ref[b, Lc:Lc + TAIL_ROWS, :]
            qks.append(y * _sigmoid(y))

            gb = gt_ref[b] + bgv
            gbs.append(gb)
            l_parts += list(_split3_f32(_log_sigmoid(gb)))
            g_t = _pad_rows(gb, GATE_W).T
            g_ts.append(g_t)
            t_parts += list(_split3_f32(_log_sigmoid(g_t[0:2 * H_B, :])))
        col_sums = _dot(tri_bf, jnp.concatenate(l_parts, axis=1).astype(BF16))
        row_sums = _dot(jnp.concatenate(t_parts, axis=0).astype(BF16), upper_bf)

        def body(b):
            gb = gbs[b]
            qk = qks[b]
            c0 = 3 * b * GATE_W
            b_col = (col_sums[:, c0:c0 + GATE_W] + col_sums[:, c0 + GATE_W:c0 + 2 * GATE_W]
                     + col_sums[:, c0 + 2 * GATE_W:c0 + 3 * GATE_W])
            r0 = 3 * b * 2 * H_B
            b_row = row_sums[r0:r0 + 2 * H_B] + row_sums[r0 + 2 * H_B:r0 + 4 * H_B] + row_sums[r0 + 4 * H_B:r0 + 6 * H_B]
            a_row_all = g_ts[b][0:H_B, :] - b_row[H_B:2 * H_B, :]

            ogv = og_ref[b].astype(F32)
            vv = vb_ref[b]
            for h in range(H_B):
                hs = slice(h * DH_B, (h + 1) * DH_B)
                qh = qk[:, hs]
                kh = qk[:, W_B + h * DH_B:W_B + (h + 1) * DH_B] * k_scale
                vh = vv[:, hs]
                qb = qh.astype(BF16)
                kb = kh.astype(BF16)
                a_row = a_row_all[h:h + 1, 0:Lc]
                bcol = b_col[:, H_B + h:H_B + h + 1]
                a_col = gb[:, h:h + 1] - bcol
                m_prev = m_ref[b, h:h + 1, 0:1]
                amat = jnp.where(tri, a_row, -jnp.inf)
                mc = jnp.maximum(amat.max(axis=-1, keepdims=True), m_prev)
                dm = jnp.exp(amat - mc)
                c_h = C_ref[b, h]
                qck = _dot_nt(qb, jnp.concatenate([c_h.astype(BF16), kb], axis=0))
                w = qck[:, DH_B:DH_B + Lc] * dm
                a_int = jnp.exp(m_prev - mc)
                num = a_int * qck[:, 0:DH_B] + _dot(w.astype(BF16), vh)
                n_h = n_ref[b, h:h + 1, :]
                den = a_int * jnp.sum(qh * n_h, axis=-1, keepdims=True) + w.sum(axis=-1, keepdims=True)
                hh = num / jnp.maximum(jnp.abs(den), jnp.exp(-(bcol + mc)))
                hh = _rms(hh, hn[:, hs]) * _sigmoid(ogv[:, hs])
                hb_ref[b, :, hs] = hh.astype(BF16)

                m_end = mc[Lc - 1:Lc, :]
                a_end = jnp.exp(m_prev - m_end)
                w_end = jnp.exp(a_col - m_end)
                vw = vh.astype(F32) * w_end
                vw_t = _pad_rows(vw, DH_B).T.astype(BF16)
                k_p = _pad_rows(kh, DH_B).astype(BF16)
                C_ref[b, h] = a_end * c_h + _dot(vw_t, k_p)
                n_ref[b, h:h + 1, :] = a_end * n_h + jnp.sum(kh * w_end, axis=0, keepdims=True)
                m_ref[b, h:h + 1, :] = jnp.broadcast_to(bcol[Lc - 1:Lc, :] + m_end, (1, GATE_W))

        for b in range(bb):
            body(b)

    w2 = 2 * W_B
    in_specs = [
        pl.BlockSpec((bb, Lc, w2), lambda i, c: (i, c, 0)),
        pl.BlockSpec((bb, Lc, W_B), lambda i, c: (i, c, 0)),
        pl.BlockSpec((bb, Lc, W_B), lambda i, c: (i, c, 0)),
        pl.BlockSpec((bb, Lc, GATE_W), lambda i, c: (i, c, 0)),
        _const_spec((CONV_W, w2)),
        _const_spec((1, w2)),
        _const_spec((1, GATE_W)),
        _const_spec((1, W_B)),
    ]
    args = [qkb, vb, og, gates, conv_w, conv_b, bg, hnorm]
    if not zero_init:
        in_specs += [
            pl.BlockSpec((bb, H_B, DH_B, DH_B), lambda i, c: (i, 0, 0, 0)),
            pl.BlockSpec((bb, H_B, DH_B), lambda i, c: (i, 0, 0)),
            pl.BlockSpec((bb, H_B, GATE_W), lambda i, c: (i, 0, 0)),
            pl.BlockSpec((bb, TAIL_ROWS, w2), lambda i, c: (i, 0, 0)),
        ]
        args += list(init)
    out_specs = [
        pl.BlockSpec((bb, Lc, W_B), lambda i, c: (i, c, 0)),
        pl.BlockSpec((bb, H_B, DH_B, DH_B), lambda i, c: (i, 0, 0, 0)),
        pl.BlockSpec((bb, H_B, DH_B), lambda i, c: (i, 0, 0)),
        pl.BlockSpec((bb, H_B, GATE_W), lambda i, c: (i, 0, 0)),
    ]
    out_shape = [
        jax.ShapeDtypeStruct((B, L, W_B), BF16),
        jax.ShapeDtypeStruct((B, H_B, DH_B, DH_B), F32),
        jax.ShapeDtypeStruct((B, H_B, DH_B), F32),
        jax.ShapeDtypeStruct((B, H_B, GATE_W), F32),
    ]
    return pl.pallas_call(
        kernel,
        grid=(B // bb, nC),
        in_specs=in_specs,
        out_specs=out_specs,
        out_shape=out_shape,
        scratch_shapes=[pltpu.VMEM((bb, TAIL_ROWS, w2), F32), pltpu.VMEM((bb, TAIL_ROWS + Lc, w2), F32)],
        compiler_params=_params(("arbitrary", "arbitrary")),
        name=name,
    )(*args)


def _outffn_call(parts, wo_parts, x, gt1, sc2, sh2, gt2, ng, ffn_w, tm, name, bb=1):
    B, L, d = x.shape
    nT = L // tm
    n_in = len(parts)
    wg, wu, wd = ffn_w
    fc = FF_CHUNK
    n_ff = wg.shape[1] // fc
    rows = bb * tm

    def kernel(*refs):
        a_refs = refs[:n_in]
        wo_refs = refs[n_in:2 * n_in]
        (x_ref, gt1_ref, sc2_ref, sh2_ref, gt2_ref, ng_ref, wg_ref, wu_ref, wd_ref,
         o_ref, h_ref, z_ref) = refs[2 * n_in:]
        half = tm // 2 if (bb == 1 and tm % 32 == 0) else tm
        for r0 in range(0, tm, half):
            rs = slice(r0, r0 + half)
            y = None
            for a_ref, wo_ref in zip(a_refs, wo_refs):
                part = _dot(a_ref[:, rs, :].reshape(bb * half, a_ref.shape[2]), wo_ref[...])
                y = part if y is None else y + part
            x1 = x_ref[:, rs, :] + gt1_ref[...] * _rms(y, ng_ref[1:2, :]).reshape(bb, half, d)
            o_ref[:, rs, :] = x1
            h = _rms(x1, ng_ref[2:3, :]) * (1.0 + sc2_ref[...]) + sh2_ref[...]
            h_ref[bb * r0:bb * (r0 + half), :] = h.reshape(bb * half, d).astype(BF16)

        for j in range(n_ff):
            cs = slice(j * fc, (j + 1) * fc)
            gpart = _dot(h_ref[...], wg_ref[:, cs])
            upart = _dot(h_ref[...], wu_ref[:, cs])
            z_ref[j] = (gpart * _sigmoid(gpart) * upart).astype(BF16)
        for r0 in range(0, tm, half):
            zr = slice(bb * r0, bb * (r0 + half))
            acc = _dot(z_ref[0, zr, :], wd_ref[0:fc, :])
            for j in range(1, n_ff):
                acc = acc + _dot(z_ref[j, zr, :], wd_ref[j * fc:(j + 1) * fc, :])
            rs = slice(r0, r0 + half)
            o_ref[:, rs, :] = o_ref[:, rs, :] + gt2_ref[...] * _rms(acc, ng_ref[3:4, :]).reshape(bb, half, d)

    mod_spec = pl.BlockSpec((bb, 1, d), lambda b, t: (b, 0, 0))
    in_specs = [pl.BlockSpec((bb, tm, p.shape[2]), lambda b, t: (b, t, 0)) for p in parts]
    in_specs += [_const_spec(w.shape) for w in wo_parts]
    in_specs += [
        pl.BlockSpec((bb, tm, d), lambda b, t: (b, t, 0)),
        mod_spec, mod_spec, mod_spec, mod_spec,
        _const_spec(ng.shape),
        _const_spec(wg.shape),
        _const_spec(wu.shape),
        _const_spec(wd.shape),
    ]
    return pl.pallas_call(
        kernel,
        grid=(B // bb, nT),
        in_specs=in_specs,
        out_specs=pl.BlockSpec((bb, tm, d), lambda b, t: (b, t, 0)),
        out_shape=jax.ShapeDtypeStruct((B, L, d), F32),
        scratch_shapes=[pltpu.VMEM((rows, d), BF16), pltpu.VMEM((n_ff, rows, fc), BF16)],
        compiler_params=_params(("arbitrary", "arbitrary")),
        name=name,
    )(*parts, *wo_parts, x, gt1, sc2, sh2, gt2, ng, wg, wu, wd)


def _perm_heads(w, axis):
    blocks = [lax.slice_in_dim(w, h * HEAD_DIM, (h + 1) * HEAD_DIM, axis=axis) for h in PERM_A]
    return jnp.concatenate(blocks, axis=axis)


def _prep_w_in_ab(w):
    o_k = W_A
    o_qk = W_A + 2 * KV_A
    o_vb = o_qk + 2 * W_B
    o_g = o_vb + W_B
    o_og = o_g + 2 * H_B
    d = w.shape[0]
    cols = [_perm_heads(w[:, :W_A], 1) * QK_SCALE, w[:, o_k:o_qk], w[:, o_qk:o_vb], w[:, o_vb:o_g],
            w[:, o_og:o_og + W_B], w[:, o_g:o_og], jnp.zeros((d, GATE_W - 2 * H_B), w.dtype)]
    return jnp.concatenate(cols, axis=1).astype(BF16)


def _prep_ffn(wg, wu, wd):
    assert wg.shape[1] % FF_CHUNK == 0
    return wg.astype(BF16), wu.astype(BF16), wd.astype(BF16)


def kernel(x_prompt, x_sample, cache_a_k, cache_a_v, state_b_c, state_b_n, state_b_m, state_b_conv, cache_c_k, cache_c_v, c_prompt, c_sample, w_in_ab, sink_a, conv_w_b, conv_b_b, b_gates_b, hnorm_b, w_out_ab, w_in_c, relbias_c, w_out_c, w_ada, b_ada, norm_g, w_ffn_gate, w_ffn_up, w_ffn_down):
    Bp, Lp, d = x_prompt.shape
    Bs, Ls, _ = x_sample.shape
    depth = w_ada.shape[0]
    tm_p = ROW_TILE
    assert Lp % ROW_TILE == 0 and Lp % (CHUNK * CHUNKS_PER_STEP_A) == 0 and Lp % (CHUNK * CHUNKS_PER_STEP_C) == 0
    assert (Bp + Bs) % 8 == 0 and Ls % 8 == 0 and Ls >= CONV_W - 1 and Bp % MLSTM_SEQS_PER_STEP == 0 and Bs % MLSTM_SEQS_PER_STEP == 0

    mods = _ada_call(jnp.concatenate([c_prompt, c_sample], axis=0), w_ada, b_ada)

    def mod_parts(i, lo, hi):
        m = mods[i, lo:hi].reshape(hi - lo, 1, 6, d)
        return [m[:, :, k, :] for k in range(6)]

    c_qa, c_kv, c_qkb = 0, W_A, W_A + 2 * KV_A
    c_vb = c_qkb + 2 * W_B
    c_og = c_vb + W_B
    c_gt = c_og + W_B
    groups_a = [(0, list(range(0, W_A, KV_A)))]
    wg_c = 4 * HEAD_DIM
    groups_c = [(o, [o]) for o in range(0, W_C, wg_c)]

    xp, xs = x_prompt, x_sample
    st_p = {k: [] for k in ("a_k", "a_v", "b_c", "b_n", "b_m", "b_conv", "c_k", "c_v")}
    st_s = {k: [] for k in st_p}

    for i in range(depth):
        j = i // 2
        ng = norm_g[i]
        g0 = ng[0:1]
        shp1, scp1, gtp1, shp2, scp2, gtp2 = mod_parts(i, 0, Bp)
        shs1, scs1, gts1, shs2, scs2, gts2 = mod_parts(i, Bp, Bp + Bs)
        ffn_w = _prep_ffn(w_ffn_gate[i], w_ffn_up[i], w_ffn_down[i])

        if i % 2 == 0:
            w_in = _prep_w_in_ab(w_in_ab[j])
            wo_parts = [_perm_heads(w_out_ab[j][:W_A], 0).astype(BF16), w_out_ab[j][W_A:].astype(BF16)]
            conv_w = conv_w_b[j]
            conv_b = conv_b_b[j].reshape(1, 2 * W_B)
            bg = jnp.pad(b_gates_b[j], (0, GATE_W - 2 * H_B)).reshape(1, GATE_W)
            hn = hnorm_b[j].reshape(1, W_B)

            segs_p = [
                (c_qa, W_A, BF16, False, 0, 1),
                (c_kv, 2 * KV_A, BF16, True, WINDOW_A, 2),
                (c_qkb, 2 * W_B, BF16, False, TAIL_ROWS, 1),
                (c_vb, W_B, BF16, False, 0, 1),
                (c_og, W_B, BF16, False, 0, 1),
                (c_gt, GATE_W, F32, False, 0, 1),
            ]
            qa, kva, qkb, vb, og, gts, k_tail, v_tail, conv_tail = _inproj_call(
                xp, scp1, shp1, g0, w_in, segs_p, tm_p, 1, "inproj_ab_p")
            out_a = _band_attn_a_call(qa, kva, _alibi_rows(sink_a[j], CHUNK).T, CHUNKS_PER_STEP_A, "attn_a_p")
            hb, s_c, s_n, s_m = _mlstm_call(qkb, vb, og, gts, conv_w, conv_b, bg, hn, None, MLSTM_CHUNK, "mlstm_p")
            xp = _outffn_call([out_a, hb], wo_parts, xp, gtp1, scp2, shp2, gtp2, ng, ffn_w, tm_p, "outffn_ab_p")
            st_p["a_k"].append(k_tail.reshape(Bp, WINDOW_A, HKV_A, HEAD_DIM))
            st_p["a_v"].append(v_tail.reshape(Bp, WINDOW_A, HKV_A, HEAD_DIM))
            st_p["b_c"].append(s_c)
            st_p["b_n"].append(s_n)
            st_p["b_m"].append(s_m[:, :, 0])
            st_p["b_conv"].append(conv_tail[:, TAIL_ROWS - (CONV_W - 1):, :])

            segs_s = [
                (c_qa, W_A, BF16, False, 0, 1),
                (c_kv, 2 * KV_A, F32, False, 0, 1),
                (c_qkb, 2 * W_B, BF16, False, TAIL_ROWS, 1),
                (c_vb, W_B, BF16, False, 0, 1),
                (c_og, W_B, BF16, False, 0, 1),
                (c_gt, GATE_W, F32, False, 0, 1),
            ]
            qa, kvn, qkb, vb, og, gts, conv_tail = _inproj_call(
                xs, scs1, shs1, g0, w_in, segs_s, Ls, 0, "inproj_ab_s", bb=Bs)
            la = cache_a_k.shape[2]
            out_a = _cache_attn_call(qa, cache_a_k[j].reshape(Bs, la, KV_A), cache_a_v[j].reshape(Bs, la, KV_A),
                                     kvn, _alibi_rows(sink_a[j], Ls), KV_A, groups_a, True, "attn_a_s")
            init = (state_b_c[j], state_b_n[j],
                    jnp.broadcast_to(state_b_m[j][:, :, None], (Bs, H_B, GATE_W)),
                    jnp.pad(state_b_conv[j], ((0, 0), (TAIL_ROWS - (CONV_W - 1), 0), (0, 0))))
            hb, s_c, s_n, s_m = _mlstm_call(qkb, vb, og, gts, conv_w, conv_b, bg, hn, init, Ls, "mlstm_s")
            xs = _outffn_call([out_a, hb], wo_parts, xs, gts1, scs2, shs2, gts2, ng, ffn_w, Ls, "outffn_ab_s", bb=Bs)
            st_s["a_k"].append(kvn[:, :, :KV_A].reshape(Bs, Ls, HKV_A, HEAD_DIM))
            st_s["a_v"].append(kvn[:, :, KV_A:].reshape(Bs, Ls, HKV_A, HEAD_DIM))
            st_s["b_c"].append(s_c)
            st_s["b_n"].append(s_n)
            st_s["b_m"].append(s_m[:, :, 0])
            st_s["b_conv"].append(conv_tail[:, TAIL_ROWS - (CONV_W - 1):, :])
        else:
            col_scale = jnp.where(jnp.arange(3 * W_C) < W_C, QK_SCALE, 1.0).astype(F32)
            w_in = (w_in_c[j] * col_scale[None, :]).astype(BF16)
            wo_parts = [w_out_c[j].astype(BF16)]
            bias = _relbias_call(relbias_c[j])
            lc = cache_c_k.shape[2]
            tail_c = NPREV_C * CHUNK

            segs_p = [
                (0, W_C, BF16, False, 0, 1),
                (W_C, 2 * W_C, BF16, True, tail_c, 2),
            ]
            qc, kvc, k_tail, v_tail = _inproj_call(xp, scp1, shp1, g0, w_in, segs_p, tm_p, 1, "inproj_c_p")
            out_c = _band_attn_call(qc, kvc, bias.reshape(H_C * CHUNK, BAND_C), wg_c, groups_c, BAND_C, False,
                                    CHUNKS_PER_STEP_C, PHASE_UNITS_C, "attn_c_p")
            xp = _outffn_call([out_c], wo_parts, xp, gtp1, scp2, shp2, gtp2, ng, ffn_w, tm_p, "outffn_c_p")
            st_p["c_k"].append(k_tail.reshape(Bp, tail_c, H_C, HEAD_DIM))
            st_p["c_v"].append(v_tail.reshape(Bp, tail_c, H_C, HEAD_DIM))

            segs_s = [
                (0, W_C, BF16, False, 0, 1),
                (W_C, 2 * W_C, F32, False, 0, 1),
            ]
            qc, kvn = _inproj_call(xs, scs1, shs1, g0, w_in, segs_s, Ls, 0, "inproj_c_s", bb=Bs)
            off = BAND_C - CHUNK - lc
            bias_c = bias[:, :Ls, off:off + lc].reshape(H_C * Ls, lc)
            bias_n = bias[:, :Ls, off + lc:off + lc + Ls].reshape(H_C * Ls, Ls)
            out_c = _cache_attn_call(qc, cache_c_k[j].reshape(Bs, lc, W_C), cache_c_v[j].reshape(Bs, lc, W_C),
                                     kvn, (bias_c, bias_n), wg_c, groups_c, False, "attn_c_s")
            xs = _outffn_call([out_c], wo_parts, xs, gts1, scs2, shs2, gts2, ng, ffn_w, Ls, "outffn_c_s", bb=Bs)
            st_s["c_k"].append(kvn[:, :, :W_C].reshape(Bs, Ls, H_C, HEAD_DIM))
            st_s["c_v"].append(kvn[:, :, W_C:].reshape(Bs, Ls, H_C, HEAD_DIM))

    order = ("a_k", "a_v", "b_c", "b_n", "b_m", "b_conv", "c_k", "c_v")
    outs = [xp, xs]
    outs += [jnp.stack(st_p[k]) for k in order]
    outs += [jnp.stack(st_s[k]) for k in order]
    return tuple(outs)
```

```python
import jax
import jax.numpy as jnp
from jax import lax
from jax.experimental import pallas as pl
from jax.experimental.pallas import tpu as pltpu

F32 = jnp.float32
BF16 = jnp.bfloat16

D_MODEL = 1024
CHUNK = 64
HEAD_DIM = 64
HQ_A = 8
HKV_A = 2
WINDOW_A = 128
H_B = 4
DH_B = 128
CONV_W = 4
H_C = 16
NPREV_C = 8
MAX_REL_C = 256
EPS = 1e-6
NEG = -1e30
W_A = HQ_A * HEAD_DIM
KV_A = HKV_A * HEAD_DIM
W_B = H_B * DH_B
W_C = H_C * HEAD_DIM
BAND_A = WINDOW_A + CHUNK
BAND_C = (NPREV_C + 1) * CHUNK
FF_CHUNK = 256
GATE_W = 128
TAIL_ROWS = 8
ROW_TILE = 512
V7X_VMEM_LIMIT = 56 * 1024 * 1024
LOG2E = 1.4426950408889634
QK_SCALE = HEAD_DIM ** -0.5 * LOG2E
PERM_A = tuple(h for pair in zip(range(HQ_A // HKV_A), range(HQ_A // HKV_A, HQ_A)) for h in pair)
CHUNKS_PER_STEP_A = 8
CHUNKS_PER_STEP_C = 2
PHASE_UNITS_A = 8
PHASE_UNITS_C = 1
MLSTM_SEQS_PER_STEP = 4
MLSTM_CHUNK = 128


def _params(sem, vmem=V7X_VMEM_LIMIT, flags=None):
    return pltpu.CompilerParams(dimension_semantics=sem, vmem_limit_bytes=vmem, flags=flags)


def _const_spec(shape):
    nd = len(shape)
    return pl.BlockSpec(shape, lambda *_: (0,) * nd, pipeline_mode=pl.Buffered(1))


def _split3(x):
    hi = x.astype(BF16)
    r1 = x - hi.astype(F32)
    mid = r1.astype(BF16)
    lo = (r1 - mid.astype(F32)).astype(BF16)
    return hi, mid, lo


def _pad_rows(x, rows):
    if x.shape[0] == rows:
        return x
    return jnp.concatenate([x, jnp.zeros((rows - x.shape[0], x.shape[1]), x.dtype)], axis=0)


def _split3_f32(x):
    return tuple(p.astype(F32) for p in _split3(x))


def _dot(a, b):
    return jnp.dot(a, b, preferred_element_type=F32)


def _dot_nt(a, b):
    return lax.dot_general(a, b, (((1,), (1,)), ((), ())), preferred_element_type=F32)


def _rms(x, g):
    return x * lax.rsqrt(jnp.mean(x * x, axis=-1, keepdims=True) + EPS) * g


def _sigmoid(x):
    return 1.0 / (1.0 + jnp.exp(-x))


def _log_sigmoid(x):
    return jnp.minimum(x, 0.0) - jnp.log(1.0 + jnp.exp(-jnp.abs(x)))


def _ada_kernel(c_ref, w_ref, b_ref, o_ref):
    c = c_ref[...]
    s = c * _sigmoid(c)
    s_hi, s_mid, _ = _split3(s)
    w_hi, w_mid, _ = _split3(w_ref[0])
    acc = _dot(s_hi, w_hi) + _dot(s_hi, w_mid) + _dot(s_mid, w_hi)
    o_ref[0] = acc + b_ref[0]


def _ada_call(c_all, w_ada, b_ada):
    depth, d, n6 = w_ada.shape
    nb = c_all.shape[0]
    tn = 512
    return pl.pallas_call(
        _ada_kernel,
        grid=(depth, n6 // tn),
        in_specs=[
            pl.BlockSpec((nb, d), lambda i, j: (0, 0)),
            pl.BlockSpec((1, d, tn), lambda i, j: (i, 0, j)),
            pl.BlockSpec((1, 1, tn), lambda i, j: (i, 0, j)),
        ],
        out_specs=pl.BlockSpec((1, nb, tn), lambda i, j: (i, 0, j)),
        out_shape=jax.ShapeDtypeStruct((depth, nb, n6), F32),
        compiler_params=_params(("arbitrary", "arbitrary")),
        name="ada_mod",
    )(c_all, w_ada, b_ada.reshape(depth, 1, n6))


def _inproj_call(x, sc, sh, g, w, segs, tm, pad, name, bb=1):
    B, L, d = x.shape
    nT = L // tm
    P = w.shape[1]
    n_seg = len(segs)

    def kernel(x_ref, sc_ref, sh_ref, g_ref, w_ref, *outs):
        seg_refs = outs[:n_seg]
        tail_refs = outs[n_seg:]
        t = pl.program_id(1)

        def compute():
            half = tm // 2 if (bb == 1 and tm % 32 == 0) else tm
            hs = []
            for r0 in range(0, tm, half):
                xs = x_ref[:, r0:r0 + half, :]
                hh = (_rms(xs, g_ref[...]) * (1.0 + sc_ref[...]) + sh_ref[...]).astype(BF16)
                hs.append(hh.reshape(bb * half, d))
            ti = 0
            for (c0, wd, dt, _, tail_rows, tail_split), r in zip(segs, seg_refs):
                ys = [_dot(hh, w_ref[:, c0:c0 + wd]) for hh in hs]
                y = (ys[0] if len(ys) == 1 else jnp.concatenate(ys, axis=0)).reshape(bb, tm, wd)
                r[...] = y.astype(dt)
                if tail_rows:
                    trs = tail_refs[ti:ti + tail_split]
                    ti += tail_split
                    ws = wd // tail_split

                    @pl.when(t == nT - 1 + pad)
                    def _():
                        for k, tr in enumerate(trs):
                            tr[...] = y[:, tm - tail_rows:, k * ws:(k + 1) * ws]

        if pad:
            @pl.when(t == 0)
            def _():
                for (_, wd, dt, padded, _, _), r in zip(segs, seg_refs):
                    if padded:
                        r[...] = jnp.zeros((bb, tm, wd), dt)

            pl.when(t > 0)(compute)
        else:
            compute()

    def row_idx(t):
        return jnp.maximum(t - 1, 0) if pad else t

    in_specs = [
        pl.BlockSpec((bb, tm, d), lambda b, t: (b, row_idx(t), 0)),
        pl.BlockSpec((bb, 1, d), lambda b, t: (b, 0, 0)),
        pl.BlockSpec((bb, 1, d), lambda b, t: (b, 0, 0)),
        _const_spec((1, d)),
        _const_spec((d, P)),
    ]
    out_specs, out_shapes = [], []
    for (_, wd, dt, padded, _, _) in segs:
        if padded:
            out_specs.append(pl.BlockSpec((bb, tm, wd), lambda b, t: (b, t, 0)))
            out_shapes.append(jax.ShapeDtypeStruct((B, tm + L, wd), dt))
        else:
            out_specs.append(pl.BlockSpec((bb, tm, wd), lambda b, t: (b, row_idx(t), 0)))
            out_shapes.append(jax.ShapeDtypeStruct((B, L, wd), dt))
    for (_, wd, _, _, tail_rows, tail_split) in segs:
        for _ in range(tail_split if tail_rows else 0):
            out_specs.append(pl.BlockSpec((bb, tail_rows, wd // tail_split), lambda b, t: (b, 0, 0)))
            out_shapes.append(jax.ShapeDtypeStruct((B, tail_rows, wd // tail_split), F32))

    return pl.pallas_call(
        kernel,
        grid=(B // bb, nT + pad),
        in_specs=in_specs,
        out_specs=out_specs,
        out_shape=out_shapes,
        compiler_params=_params(("arbitrary", "arbitrary")),
        name=name,
    )(x, sc, sh, g, w)


def _attend_grouped(q_tiles, pieces, sink_col, lq):
    return _attend_units([(q_tiles, pieces, sink_col)], lq)[0]


def _attend_units(units, lq):
    wg = units[0][0][0].shape[1]
    nb = wg // HEAD_DIM
    blk = lax.shift_right_logical(lax.broadcasted_iota(jnp.int32, (lq, wg), 1), HEAD_DIM.bit_length() - 1)
    scores = []
    for q_tiles, pieces, _ in units:
        rows = []
        for qt in q_tiles:
            for r in range(nb):
                rows.append(jnp.where(blk == r, qt, 0.0).astype(BF16))
        qbd = jnp.concatenate(rows, axis=0)
        ss = []
        for k, _, bias, valid in pieces:
            s = _dot_nt(qbd, k) + bias
            if valid is not None:
                s = jnp.where(valid, s, NEG)
            ss.append(s)
        scores.append(ss)
    maxima = []
    for ss, (_, _, sink_col) in zip(scores, units):
        m = ss[0].max(axis=-1, keepdims=True)
        for s in ss[1:]:
            m = jnp.maximum(m, s.max(axis=-1, keepdims=True))
        if sink_col is not None:
            m = jnp.maximum(m, sink_col)
        maxima.append(m)
    probs, dens = [], []
    for ss, m, (_, _, sink_col) in zip(scores, maxima, units):
        ps = [jnp.exp2(s - m) for s in ss]
        den = ps[0].sum(axis=-1, keepdims=True)
        for p in ps[1:]:
            den = den + p.sum(axis=-1, keepdims=True)
        if sink_col is not None:
            den = den + jnp.exp2(sink_col - m)
        probs.append([p.astype(BF16) for p in ps])
        dens.append(den)
    results = []
    for ps, den, (q_tiles, pieces, _) in zip(probs, dens, units):
        o = _dot(ps[0], pieces[0][1])
        for p, piece in zip(ps[1:], pieces[1:]):
            o = o + _dot(p, piece[1])
        o = o / den
        outs = []
        for t in range(len(q_tiles)):
            base = t * nb * lq
            acc = o[base + (nb - 1) * lq:base + nb * lq]
            for r in range(nb - 2, -1, -1):
                acc = jnp.where(blk == r, o[base + r * lq:base + (r + 1) * lq], acc)
            outs.append(acc)
        results.append(outs)
    return results


def _alibi_rows(sinks, lq):
    perm = jnp.array(PERM_A)
    slopes = jnp.exp2(-8.0 * jnp.arange(1, HQ_A + 1, dtype=F32) / HQ_A)
    rc = jnp.stack([-slopes[perm], sinks.astype(F32)[perm]], axis=1) * LOG2E
    return jnp.repeat(rc, lq, axis=0)


def _band_attn_a_call(q, kv, rowc, nq, name):
    B, L, wq = q.shape
    nsteps = L // (CHUNK * nq)
    band, hist = BAND_A, WINDOW_A
    kp = 2 * KV_A
    m_lanes = HQ_A * CHUNK
    n_tiles = W_A // KV_A

    def kernel(q_ref, kv_ref, e_ref, o_ref):
        step = pl.program_id(1)
        blk = lax.shift_right_logical(lax.broadcasted_iota(jnp.int32, (CHUNK, KV_A), 1), HEAD_DIM.bit_length() - 1)
        jj = lax.broadcasted_iota(jnp.int32, (band, m_lanes), 0)
        ii = lax.broadcasted_iota(jnp.int32, (band, m_lanes), 1) & (CHUNK - 1)
        bias = e_ref[0:1, :] * jnp.abs(ii + hist - jj).astype(F32)
        sink = e_ref[1:2, :]
        zeros_v = jnp.zeros((kp - band, KV_A), F32)
        zeros_p = jnp.zeros((kp - band, m_lanes), F32)

        def run(masked):
            scores, vts = [], []
            for cc in range(nq):
                c = step * nq + cc
                start = pl.multiple_of(c * CHUNK + ROW_TILE - hist, CHUNK)
                rs = slice(cc * CHUNK, (cc + 1) * CHUNK)
                rows = []
                for t in range(n_tiles):
                    qt = q_ref[0, rs, t * KV_A:(t + 1) * KV_A].astype(F32)
                    for r in range(HKV_A):
                        rows.append(jnp.where(blk == r, qt, 0.0).astype(BF16))
                qbd = jnp.concatenate(rows, axis=0)
                s = _dot_nt(kv_ref[0, pl.ds(start, band), 0:KV_A], qbd) + bias
                if masked:
                    s = jnp.where((c * CHUNK - hist + jj) >= 0, s, NEG)
                scores.append(s)
                v = kv_ref[0, pl.ds(start, band), KV_A:2 * KV_A].astype(F32)
                vts.append(jnp.concatenate([v, zeros_v], axis=0).T.astype(BF16))
            maxima = [jnp.maximum(s.max(axis=0, keepdims=True), sink) for s in scores]
            probs = []
            for s, m in zip(scores, maxima):
                p = jnp.exp2(s - m)
                den = p.sum(axis=0, keepdims=True) + jnp.exp2(sink - m)
                probs.append(jnp.concatenate([p / den, zeros_p], axis=0).astype(BF16))
            for cc, (vt, pn) in enumerate(zip(vts, probs)):
                o = _dot(vt, pn).T
                rs = slice(cc * CHUNK, (cc + 1) * CHUNK)
                for t in range(n_tiles):
                    base = t * HKV_A * CHUNK
                    acc = o[base + (HKV_A - 1) * CHUNK:base + HKV_A * CHUNK]
                    for r in range(HKV_A - 2, -1, -1):
                        acc = jnp.where(blk == r, o[base + r * CHUNK:base + (r + 1) * CHUNK], acc)
                    o_ref[0, rs, t * KV_A:(t + 1) * KV_A] = acc.astype(BF16)

        reaches_pad = step * (nq * CHUNK) < hist
        pl.when(reaches_pad)(lambda: run(True))
        pl.when(jnp.logical_not(reaches_pad))(lambda: run(False))

    return pl.pallas_call(
        kernel,
        grid=(B, nsteps),
        in_specs=[
            pl.BlockSpec((1, nq * CHUNK, wq), lambda b, c: (b, c, 0)),
            pl.BlockSpec((1, ROW_TILE + L, kv.shape[2]), lambda b, c: (b, 0, 0)),
            _const_spec(rowc.shape),
        ],
        out_specs=pl.BlockSpec((1, nq * CHUNK, wq), lambda b, c: (b, c, 0)),
        out_shape=jax.ShapeDtypeStruct((B, L, wq), BF16),
        compiler_params=_params(("arbitrary", "arbitrary")),
        name=name,
    )(q, kv, rowc)


def _band_attn_call(q, kv, extra, wg, q_groups, band, alibi, nq, phase_units, name):
    B, L, wq = q.shape
    nsteps = L // (CHUNK * nq)
    hist = band - CHUNK
    kvw = kv.shape[2]
    v_off = kvw // 2
    nb = wg // HEAD_DIM

    def kernel(q_ref, kv_ref, e_ref, o_ref):
        step = pl.program_id(1)
        jj = lax.broadcasted_iota(jnp.int32, (1, band), 1)
        if alibi:
            m_rows = len(q_groups[0][1]) * nb * CHUNK
            ii = lax.broadcasted_iota(jnp.int32, (m_rows, band), 0) & (CHUNK - 1)
            dist = jnp.abs(ii + hist - lax.broadcasted_iota(jnp.int32, (m_rows, band), 1)).astype(F32)
        rows_of = []
        bias_of = []
        for _, q_offs in q_groups:
            r0 = (q_offs[0] // HEAD_DIM) * CHUNK
            r1 = r0 + len(q_offs) * nb * CHUNK
            rows_of.append((r0, r1))
            bias_of.append(e_ref[r0:r1, 0:1] * dist if alibi else None)
        def run(masked):
            units, dests = [], []
            for cc in range(nq):
                c = step * nq + cc
                start = pl.multiple_of(c * CHUNK + ROW_TILE - hist, CHUNK)
                valid = ((c * CHUNK - hist + jj) >= 0) if masked else None
                rs = slice(cc * CHUNK, (cc + 1) * CHUNK)
                for gi, (kv0, q_offs) in enumerate(q_groups):
                    kb = kv_ref[0, pl.ds(start, band), kv0:kv0 + wg]
                    vb = kv_ref[0, pl.ds(start, band), v_off + kv0:v_off + kv0 + wg]
                    tiles = [q_ref[0, rs, o:o + wg].astype(F32) for o in q_offs]
                    r0, r1 = rows_of[gi]
                    if alibi:
                        units.append((tiles, [(kb, vb, bias_of[gi], valid)], e_ref[r0:r1, 1:2]))
                    else:
                        units.append((tiles, [(kb, vb, e_ref[r0:r1, :], valid)], None))
                    dests.append((rs, q_offs))
            for u0 in range(0, len(units), phase_units):
                for (rs, q_offs), outs in zip(dests[u0:u0 + phase_units],
                                              _attend_units(units[u0:u0 + phase_units], CHUNK)):
                    for o, res in zip(q_offs, outs):
                        o_ref[0, rs, o:o + wg] = res.astype(BF16)

        reaches_pad = step * (nq * CHUNK) < hist
        pl.when(reaches_pad)(lambda: run(True))
        pl.when(jnp.logical_not(reaches_pad))(lambda: run(False))

    return pl.pallas_call(
        kernel,
        grid=(B, nsteps),
        in_specs=[
            pl.BlockSpec((1, nq * CHUNK, wq), lambda b, c: (b, c, 0)),
            pl.BlockSpec((1, ROW_TILE + L, kvw), lambda b, c: (b, 0, 0)),
            _const_spec(extra.shape),
        ],
        out_specs=pl.BlockSpec((1, nq * CHUNK, wq), lambda b, c: (b, c, 0)),
        out_shape=jax.ShapeDtypeStruct((B, L, wq), BF16),
        compiler_params=_params(("arbitrary", "arbitrary")),
        name=name,
    )(q, kv, extra)


def _cache_attn_call(q, kc, vc, kvn, extra, wg, q_groups, alibi, name):
    B, T, wq = q.shape
    Lc = kc.shape[1]
    v_off = kvn.shape[2] // 2
    nb = wg // HEAD_DIM
    assert T & (T - 1) == 0

    def kernel(q_ref, kc_ref, vc_ref, kvn_ref, *rest):
        m_rows = len(q_groups[0][1]) * nb * T
        if alibi:
            e_ref, o_ref = rest
            ii = lax.broadcasted_iota(jnp.int32, (m_rows, Lc), 0) & (T - 1)
            dist_c = jnp.abs(ii + Lc - lax.broadcasted_iota(jnp.int32, (m_rows, Lc), 1)).astype(F32)
            ii = lax.broadcasted_iota(jnp.int32, (m_rows, T), 0) & (T - 1)
            dist_n = jnp.abs(ii - lax.broadcasted_iota(jnp.int32, (m_rows, T), 1)).astype(F32)
        else:
            bc_ref, bn_ref, o_ref = rest
        for kv0, q_offs in q_groups:
            k_c = kc_ref[0, :, kv0:kv0 + wg].astype(BF16)
            v_c = vc_ref[0, :, kv0:kv0 + wg].astype(BF16)
            k_n = kvn_ref[0, :, kv0:kv0 + wg].astype(BF16)
            v_n = kvn_ref[0, :, v_off + kv0:v_off + kv0 + wg].astype(BF16)
            tiles = [q_ref[0, :, o:o + wg].astype(F32) for o in q_offs]
            r0 = (q_offs[0] // HEAD_DIM) * T
            r1 = r0 + m_rows
            if alibi:
                slope = e_ref[r0:r1, 0:1]
                outs = _attend_grouped(tiles, [(k_c, v_c, slope * dist_c, None), (k_n, v_n, slope * dist_n, None)],
                                       e_ref[r0:r1, 1:2], T)
            else:
                outs = _attend_grouped(tiles, [(k_c, v_c, bc_ref[r0:r1, :], None),
                                               (k_n, v_n, bn_ref[r0:r1, :], None)], None, T)
            for o, res in zip(q_offs, outs):
                o_ref[0, :, o:o + wg] = res.astype(BF16)

    in_specs = [
        pl.BlockSpec((1, T, wq), lambda b: (b, 0, 0)),
        pl.BlockSpec((1, Lc, kc.shape[2]), lambda b: (b, 0, 0)),
        pl.BlockSpec((1, Lc, vc.shape[2]), lambda b: (b, 0, 0)),
        pl.BlockSpec((1, T, kvn.shape[2]), lambda b: (b, 0, 0)),
    ]
    if alibi:
        in_specs.append(_const_spec(extra.shape))
        args = (q, kc, vc, kvn, extra)
    else:
        in_specs += [_const_spec(extra[0].shape), _const_spec(extra[1].shape)]
        args = (q, kc, vc, kvn, extra[0], extra[1])
    return pl.pallas_call(
        kernel,
        grid=(B,),
        in_specs=in_specs,
        out_specs=pl.BlockSpec((1, T, wq), lambda b: (b, 0, 0)),
        out_shape=jax.ShapeDtypeStruct((B, T, wq), BF16),
        compiler_params=_params(("arbitrary",)),
        name=name,
    )(*args)


def _relbias_kernel(tab_ref, o_ref):
    nt = tab_ref.shape[1]
    hi, mid, lo = _split3(tab_ref[...])
    tt = lax.broadcasted_iota(jnp.int32, (nt, BAND_C), 0)
    jj = lax.broadcasted_iota(jnp.int32, (nt, BAND_C), 1)

    def body(i, carry):
        idx = jnp.clip(i - jj + NPREV_C * CHUNK, -MAX_REL_C, MAX_REL_C) + MAX_REL_C
        oh = jnp.where(tt == idx, 1.0, 0.0).astype(BF16)
        o_ref[i] = (_dot(hi, oh) + _dot(mid, oh) + _dot(lo, oh)) * LOG2E
        return carry

    lax.fori_loop(0, CHUNK, body, 0)


def _relbias_call(table):
    nh, nt = table.shape
    ntp = ((nt + 127) // 128) * 128
    tab = jnp.pad(table, ((0, 0), (0, ntp - nt)))
    out = pl.pallas_call(
        _relbias_kernel,
        out_shape=jax.ShapeDtypeStruct((CHUNK, nh, BAND_C), F32),
        compiler_params=_params(None),
        name="relbias",
    )(tab)
    return jnp.transpose(out, (1, 0, 2))


def _mlstm_call(qkb, vb, og, gates, conv_w, conv_b, bg, hnorm, init, Lc, name, bb=MLSTM_SEQS_PER_STEP):
    B, L, _ = qkb.shape
    nC = L // Lc
    zero_init = init is None
    k_scale = DH_B ** -0.5

    def kernel(*refs):
        if zero_init:
            (qkb_ref, vb_ref, og_ref, gt_ref, cw_ref, cb_ref, bg_ref, hn_ref,
             hb_ref, C_ref, n_ref, m_ref, tail_ref, ext_ref) = refs
        else:
            (qkb_ref, vb_ref, og_ref, gt_ref, cw_ref, cb_ref, bg_ref, hn_ref,
             C0_ref, n0_ref, m0_ref, conv0_ref,
             hb_ref, C_ref, n_ref, m_ref, tail_ref, ext_ref) = refs
        c = pl.program_id(1)

        @pl.when(c == 0)
        def _():
            if zero_init:
                C_ref[...] = jnp.zeros(C_ref.shape, F32)
                n_ref[...] = jnp.zeros(n_ref.shape, F32)
                m_ref[...] = jnp.zeros(m_ref.shape, F32)
                tail_ref[...] = jnp.zeros(tail_ref.shape, F32)
            else:
                C_ref[...] = C0_ref[...]
                n_ref[...] = n0_ref[...]
                m_ref[...] = m0_ref[...]
                tail_ref[...] = conv0_ref[...]

        r_i = lax.broadcasted_iota(jnp.int32, (Lc, Lc), 0)
        c_i = lax.broadcasted_iota(jnp.int32, (Lc, Lc), 1)
        tri = r_i >= c_i
        tri_t = r_i <= c_i
        tri_bf = jnp.where(tri, 1.0, 0.0).astype(BF16)
        r_u = lax.broadcasted_iota(jnp.int32, (GATE_W, GATE_W), 0)
        c_u = lax.broadcasted_iota(jnp.int32, (GATE_W, GATE_W), 1)
        upper_bf = jnp.where(r_u <= c_u, 1.0, 0.0).astype(BF16)
        cw = cw_ref[...]
        cb = cb_ref[...]
        bgv = bg_ref[...]
        hn = hn_ref[...]

        gbs, g_ts, qks, l_parts, t_parts = [], [], [], [], []
        for b in range(bb):
            u = qkb_ref[b].astype(F32)
            ext_ref[b, 0:TAIL_ROWS, :] = tail_ref[b]
            ext_ref[b, TAIL_ROWS:TAIL_ROWS + Lc, :] = u
            y = cb + cw[CONV_W - 1:CONV_W] * u
            for j in range(CONV_W - 1):
                y = y + cw[j:j + 1] * ext_ref[b, pl.ds(TAIL_ROWS - (CONV_W - 1) + j, Lc), :]
            tail_ref[b] = ext_ref[b, Lc:Lc + TAIL_ROWS, :]
            qks.append(y * _sigmoid(y))

            gb = gt_ref[b] + bgv
            gbs.append(gb)
            l_parts += list(_split3_f32(_log_sigmoid(gb)))
            g_t = _pad_rows(gb, GATE_W).T
            g_ts.append(g_t)
            t_parts += list(_split3_f32(_log_sigmoid(g_t[0:2 * H_B, :])))
        col_sums = _dot(tri_bf, jnp.concatenate(l_parts, axis=1).astype(BF16))
        row_sums = _dot(jnp.concatenate(t_parts, axis=0).astype(BF16), upper_bf)

        def body(b):
            gb = gbs[b]
            qk = qks[b]
            c0 = 3 * b * GATE_W
            b_col = (col_sums[:, c0:c0 + GATE_W] + col_sums[:, c0 + GATE_W:c0 + 2 * GATE_W]
                     + col_sums[:, c0 + 2 * GATE_W:c0 + 3 * GATE_W])
            r0 = 3 * b * 2 * H_B
            b_row = row_sums[r0:r0 + 2 * H_B] + row_sums[r0 + 2 * H_B:r0 + 4 * H_B] + row_sums[r0 + 4 * H_B:r0 + 6 * H_B]
            a_row_all = g_ts[b][0:H_B, :] - b_row[H_B:2 * H_B, :]

            ogv = og_ref[b].astype(F32)
            vv = vb_ref[b]
            for h in range(H_B):
                hs = slice(h * DH_B, (h + 1) * DH_B)
                qh = qk[:, hs]
                kh = qk[:, W_B + h * DH_B:W_B + (h + 1) * DH_B] * k_scale
                vh = vv[:, hs]
                qb = qh.astype(BF16)
                kb = kh.astype(BF16)
                a_row = a_row_all[h:h + 1, 0:Lc]
                bcol = b_col[:, H_B + h:H_B + h + 1]
                a_col = gb[:, h:h + 1] - bcol
                m_prev = m_ref[b, h:h + 1, 0:1]
                if Lc == DH_B:
                    brow = b_row[H_B + h:H_B + h + 1, :]
                    c_h = C_ref[b, h]
                    n_h = n_ref[b, h:h + 1, :]
                    amat_t = jnp.where(tri_t, a_col, -jnp.inf)
                    mc = jnp.maximum(amat_t.max(axis=0, keepdims=True), m_prev)
                    dm_t = jnp.exp(amat_t - mc)
                    lhs = jnp.concatenate([c_h, kh, _pad_rows(n_h, 16)], axis=0).astype(BF16)
                    qck_t = _dot_nt(lhs, qb)
                    w_t = qck_t[DH_B:DH_B + Lc] * dm_t
                    a_int = jnp.exp(m_prev - mc)
                    v_t = vh.astype(F32).T
                    num_t = a_int * qck_t[0:DH_B] + _dot(v_t.astype(BF16), w_t.astype(BF16))
                    den = a_int * qck_t[DH_B + Lc:DH_B + Lc + 1] + w_t.sum(axis=0, keepdims=True)
                    h_t = num_t / jnp.maximum(jnp.abs(den), jnp.exp(-(brow + mc)))
                    h_t = h_t * lax.rsqrt(jnp.mean(h_t * h_t, axis=0, keepdims=True) + EPS)
                    hb_ref[b, :, hs] = (h_t.T * hn[:, hs] * _sigmoid(ogv[:, hs])).astype(BF16)

                    m_end = mc[:, Lc - 1:Lc]
                    a_end = jnp.exp(m_prev - m_end)
                    w_end = jnp.exp(a_row - m_end)
                    upd = _dot(jnp.concatenate([v_t * w_end, _pad_rows(w_end, 16)], axis=0).astype(BF16), kb)
                    C_ref[b, h] = a_end * c_h + upd[0:DH_B]
                    n_ref[b, h:h + 1, :] = a_end * n_h + upd[DH_B:DH_B + 1]
                    m_ref[b, h:h + 1, :] = jnp.broadcast_to(brow[:, Lc - 1:Lc] + m_end, (1, GATE_W))
                    continue
                amat = jnp.where(tri, a_row, -jnp.inf)
                mc = jnp.maximum(amat.max(axis=-1, keepdims=True), m_prev)
                dm = jnp.exp(amat - mc)
                c_h = C_ref[b, h]
                qck = _dot_nt(qb, jnp.concatenate([c_h.astype(BF16), kb], axis=0))
                w = qck[:, DH_B:DH_B + Lc] * dm
                a_int = jnp.exp(m_prev - mc)
                num = a_int * qck[:, 0:DH_B] + _dot(w.astype(BF16), vh)
                n_h = n_ref[b, h:h + 1, :]
                den = a_int * jnp.sum(qh * n_h, axis=-1, keepdims=True) + w.sum(axis=-1, keepdims=True)
                hh = num / jnp.maximum(jnp.abs(den), jnp.exp(-(bcol + mc)))
                hh = _rms(hh, hn[:, hs]) * _sigmoid(ogv[:, hs])
                hb_ref[b, :, hs] = hh.astype(BF16)

                m_end = mc[Lc - 1:Lc, :]
                a_end = jnp.exp(m_prev - m_end)
                w_end = jnp.exp(a_col - m_end)
                vw = vh.astype(F32) * w_end
                vw_t = _pad_rows(vw, DH_B).T.astype(BF16)
                k_p = _pad_rows(kh, DH_B).astype(BF16)
                C_ref[b, h] = a_end * c_h + _dot(vw_t, k_p)
                n_ref[b, h:h + 1, :] = a_end * n_h + jnp.sum(kh * w_end, axis=0, keepdims=True)
                m_ref[b, h:h + 1, :] = jnp.broadcast_to(bcol[Lc - 1:Lc, :] + m_end, (1, GATE_W))

        for b in range(bb):
            body(b)

    w2 = 2 * W_B
    in_specs = [
        pl.BlockSpec((bb, Lc, w2), lambda i, c: (i, c, 0)),
        pl.BlockSpec((bb, Lc, W_B), lambda i, c: (i, c, 0)),
        pl.BlockSpec((bb, Lc, W_B), lambda i, c: (i, c, 0)),
        pl.BlockSpec((bb, Lc, GATE_W), lambda i, c: (i, c, 0)),
        _const_spec((CONV_W, w2)),
        _const_spec((1, w2)),
        _const_spec((1, GATE_W)),
        _const_spec((1, W_B)),
    ]
    args = [qkb, vb, og, gates, conv_w, conv_b, bg, hnorm]
    if not zero_init:
        in_specs += [
            pl.BlockSpec((bb, H_B, DH_B, DH_B), lambda i, c: (i, 0, 0, 0)),
            pl.BlockSpec((bb, H_B, DH_B), lambda i, c: (i, 0, 0)),
            pl.BlockSpec((bb, H_B, GATE_W), lambda i, c: (i, 0, 0)),
            pl.BlockSpec((bb, TAIL_ROWS, w2), lambda i, c: (i, 0, 0)),
        ]
        args += list(init)
    out_specs = [
        pl.BlockSpec((bb, Lc, W_B), lambda i, c: (i, c, 0)),
        pl.BlockSpec((bb, H_B, DH_B, DH_B), lambda i, c: (i, 0, 0, 0)),
        pl.BlockSpec((bb, H_B, DH_B), lambda i, c: (i, 0, 0)),
        pl.BlockSpec((bb, H_B, GATE_W), lambda i, c: (i, 0, 0)),
    ]
    out_shape = [
        jax.ShapeDtypeStruct((B, L, W_B), BF16),
        jax.ShapeDtypeStruct((B, H_B, DH_B, DH_B), F32),
        jax.ShapeDtypeStruct((B, H_B, DH_B), F32),
        jax.ShapeDtypeStruct((B, H_B, GATE_W), F32),
    ]
    return pl.pallas_call(
        kernel,
        grid=(B // bb, nC),
        in_specs=in_specs,
        out_specs=out_specs,
        out_shape=out_shape,
        scratch_shapes=[pltpu.VMEM((bb, TAIL_ROWS, w2), F32), pltpu.VMEM((bb, TAIL_ROWS + Lc, w2), F32)],
        compiler_params=_params(("arbitrary", "arbitrary")),
        name=name,
    )(*args)


def _outffn_call(parts, wo_parts, x, gt1, sc2, sh2, gt2, ng, ffn_w, tm, name, bb=1):
    B, L, d = x.shape
    nT = L // tm
    n_in = len(parts)
    wg, wu, wd = ffn_w
    fc = FF_CHUNK
    n_ff = wg.shape[1] // fc
    rows = bb * tm

    def kernel(*refs):
        a_refs = refs[:n_in]
        wo_refs = refs[n_in:2 * n_in]
        (x_ref, gt1_ref, sc2_ref, sh2_ref, gt2_ref, ng_ref, wg_ref, wu_ref, wd_ref,
         o_ref, h_ref, z_ref) = refs[2 * n_in:]
        half = tm // 2 if (bb == 1 and tm % 32 == 0) else tm
        for r0 in range(0, tm, half):
            rs = slice(r0, r0 + half)
            y = None
            for a_ref, wo_ref in zip(a_refs, wo_refs):
                part = _dot(a_ref[:, rs, :].reshape(bb * half, a_ref.shape[2]), wo_ref[...])
                y = part if y is None else y + part
            x1 = x_ref[:, rs, :] + gt1_ref[...] * _rms(y, ng_ref[1:2, :]).reshape(bb, half, d)
            o_ref[:, rs, :] = x1
            h = _rms(x1, ng_ref[2:3, :]) * (1.0 + sc2_ref[...]) + sh2_ref[...]
            h_ref[bb * r0:bb * (r0 + half), :] = h.reshape(bb * half, d).astype(BF16)

        for j in range(n_ff):
            cs = slice(j * fc, (j + 1) * fc)
            gpart = _dot(h_ref[...], wg_ref[:, cs])
            upart = _dot(h_ref[...], wu_ref[:, cs])
            z_ref[j] = (gpart * _sigmoid(gpart) * upart).astype(BF16)
        for r0 in range(0, tm, half):
            zr = slice(bb * r0, bb * (r0 + half))
            acc = _dot(z_ref[0, zr, :], wd_ref[0:fc, :])
            for j in range(1, n_ff):
                acc = acc + _dot(z_ref[j, zr, :], wd_ref[j * fc:(j + 1) * fc, :])
            rs = slice(r0, r0 + half)
            o_ref[:, rs, :] = o_ref[:, rs, :] + gt2_ref[...] * _rms(acc, ng_ref[3:4, :]).reshape(bb, half, d)

    mod_spec = pl.BlockSpec((bb, 1, d), lambda b, t: (b, 0, 0))
    in_specs = [pl.BlockSpec((bb, tm, p.shape[2]), lambda b, t: (b, t, 0)) for p in parts]
    in_specs += [_const_spec(w.shape) for w in wo_parts]
    in_specs += [
        pl.BlockSpec((bb, tm, d), lambda b, t: (b, t, 0)),
        mod_spec, mod_spec, mod_spec, mod_spec,
        _const_spec(ng.shape),
        _const_spec(wg.shape),
        _const_spec(wu.shape),
        _const_spec(wd.shape),
    ]
    return pl.pallas_call(
        kernel,
        grid=(B // bb, nT),
        in_specs=in_specs,
        out_specs=pl.BlockSpec((bb, tm, d), lambda b, t: (b, t, 0)),
        out_shape=jax.ShapeDtypeStruct((B, L, d), F32),
        scratch_shapes=[pltpu.VMEM((rows, d), BF16), pltpu.VMEM((n_ff, rows, fc), BF16)],
        compiler_params=_params(("arbitrary", "arbitrary")),
        name=name,
    )(*parts, *wo_parts, x, gt1, sc2, sh2, gt2, ng, wg, wu, wd)


def _perm_heads(w, axis):
    blocks = [lax.slice_in_dim(w, h * HEAD_DIM, (h + 1) * HEAD_DIM, axis=axis) for h in PERM_A]
    return jnp.concatenate(blocks, axis=axis)


def _prep_w_in_ab(w):
    o_k = W_A
    o_qk = W_A + 2 * KV_A
    o_vb = o_qk + 2 * W_B
    o_g = o_vb + W_B
    o_og = o_g + 2 * H_B
    d = w.shape[0]
    cols = [_perm_heads(w[:, :W_A], 1) * QK_SCALE, w[:, o_k:o_qk], w[:, o_qk:o_vb], w[:, o_vb:o_g],
            w[:, o_og:o_og + W_B], w[:, o_g:o_og], jnp.zeros((d, GATE_W - 2 * H_B), w.dtype)]
    return jnp.concatenate(cols, axis=1).astype(BF16)


def _prep_ffn(wg, wu, wd):
    assert wg.shape[1] % FF_CHUNK == 0
    return wg.astype(BF16), wu.astype(BF16), wd.astype(BF16)


def kernel(x_prompt, x_sample, cache_a_k, cache_a_v, state_b_c, state_b_n, state_b_m, state_b_conv, cache_c_k, cache_c_v, c_prompt, c_sample, w_in_ab, sink_a, conv_w_b, conv_b_b, b_gates_b, hnorm_b, w_out_ab, w_in_c, relbias_c, w_out_c, w_ada, b_ada, norm_g, w_ffn_gate, w_ffn_up, w_ffn_down):
    Bp, Lp, d = x_prompt.shape
    Bs, Ls, _ = x_sample.shape
    depth = w_ada.shape[0]
    tm_p = ROW_TILE
    assert Lp % ROW_TILE == 0 and Lp % (CHUNK * CHUNKS_PER_STEP_A) == 0 and Lp % (CHUNK * CHUNKS_PER_STEP_C) == 0
    assert (Bp + Bs) % 8 == 0 and Ls % 8 == 0 and Ls >= CONV_W - 1 and Bp % MLSTM_SEQS_PER_STEP == 0 and Bs % MLSTM_SEQS_PER_STEP == 0

    mods = _ada_call(jnp.concatenate([c_prompt, c_sample], axis=0), w_ada, b_ada)

    def mod_parts(i, lo, hi):
        m = mods[i, lo:hi].reshape(hi - lo, 1, 6, d)
        return [m[:, :, k, :] for k in range(6)]

    c_qa, c_kv, c_qkb = 0, W_A, W_A + 2 * KV_A
    c_vb = c_qkb + 2 * W_B
    c_og = c_vb + W_B
    c_gt = c_og + W_B
    groups_a = [(0, list(range(0, W_A, KV_A)))]
    wg_c = 4 * HEAD_DIM
    groups_c = [(o, [o]) for o in range(0, W_C, wg_c)]

    xp, xs = x_prompt, x_sample
    st_p = {k: [] for k in ("a_k", "a_v", "b_c", "b_n", "b_m", "b_conv", "c_k", "c_v")}
    st_s = {k: [] for k in st_p}

    for i in range(depth):
        j = i // 2
        ng = norm_g[i]
        g0 = ng[0:1]
        shp1, scp1, gtp1, shp2, scp2, gtp2 = mod_parts(i, 0, Bp)
        shs1, scs1, gts1, shs2, scs2, gts2 = mod_parts(i, Bp, Bp + Bs)
        ffn_w = _prep_ffn(w_ffn_gate[i], w_ffn_up[i], w_ffn_down[i])

        if i % 2 == 0:
            w_in = _prep_w_in_ab(w_in_ab[j])
            wo_parts = [_perm_heads(w_out_ab[j][:W_A], 0).astype(BF16), w_out_ab[j][W_A:].astype(BF16)]
            conv_w = conv_w_b[j]
            conv_b = conv_b_b[j].reshape(1, 2 * W_B)
            bg = jnp.pad(b_gates_b[j], (0, GATE_W - 2 * H_B)).reshape(1, GATE_W)
            hn = hnorm_b[j].reshape(1, W_B)

            segs_p = [
                (c_qa, W_A, BF16, False, 0, 1),
                (c_kv, 2 * KV_A, BF16, True, WINDOW_A, 2),
                (c_qkb, 2 * W_B, BF16, False, TAIL_ROWS, 1),
                (c_vb, W_B, BF16, False, 0, 1),
                (c_og, W_B, BF16, False, 0, 1),
                (c_gt, GATE_W, F32, False, 0, 1),
            ]
            qa, kva, qkb, vb, og, gts, k_tail, v_tail, conv_tail = _inproj_call(
                xp, scp1, shp1, g0, w_in, segs_p, tm_p, 1, "inproj_ab_p")
            out_a = _band_attn_a_call(qa, kva, _alibi_rows(sink_a[j], CHUNK).T, CHUNKS_PER_STEP_A, "attn_a_p")
            hb, s_c, s_n, s_m = _mlstm_call(qkb, vb, og, gts, conv_w, conv_b, bg, hn, None, MLSTM_CHUNK, "mlstm_p")
            xp = _outffn_call([out_a, hb], wo_parts, xp, gtp1, scp2, shp2, gtp2, ng, ffn_w, tm_p, "outffn_ab_p")
            st_p["a_k"].append(k_tail.reshape(Bp, WINDOW_A, HKV_A, HEAD_DIM))
            st_p["a_v"].append(v_tail.reshape(Bp, WINDOW_A, HKV_A, HEAD_DIM))
            st_p["b_c"].append(s_c)
            st_p["b_n"].append(s_n)
            st_p["b_m"].append(s_m[:, :, 0])
            st_p["b_conv"].append(conv_tail[:, TAIL_ROWS - (CONV_W - 1):, :])

            segs_s = [
                (c_qa, W_A, BF16, False, 0, 1),
                (c_kv, 2 * KV_A, F32, False, 0, 1),
                (c_qkb, 2 * W_B, BF16, False, TAIL_ROWS, 1),
                (c_vb, W_B, BF16, False, 0, 1),
                (c_og, W_B, BF16, False, 0, 1),
                (c_gt, GATE_W, F32, False, 0, 1),
            ]
            qa, kvn, qkb, vb, og, gts, conv_tail = _inproj_call(
                xs, scs1, shs1, g0, w_in, segs_s, Ls, 0, "inproj_ab_s", bb=Bs)
            la = cache_a_k.shape[2]
            out_a = _cache_attn_call(qa, cache_a_k[j].reshape(Bs, la, KV_A), cache_a_v[j].reshape(Bs, la, KV_A),
                                     kvn, _alibi_rows(sink_a[j], Ls), KV_A, groups_a, True, "attn_a_s")
            init = (state_b_c[j], state_b_n[j],
                    jnp.broadcast_to(state_b_m[j][:, :, None], (Bs, H_B, GATE_W)),
                    jnp.pad(state_b_conv[j], ((0, 0), (TAIL_ROWS - (CONV_W - 1), 0), (0, 0))))
            hb, s_c, s_n, s_m = _mlstm_call(qkb, vb, og, gts, conv_w, conv_b, bg, hn, init, Ls, "mlstm_s")
            xs = _outffn_call([out_a, hb], wo_parts, xs, gts1, scs2, shs2, gts2, ng, ffn_w, Ls, "outffn_ab_s", bb=Bs)
            st_s["a_k"].append(kvn[:, :, :KV_A].reshape(Bs, Ls, HKV_A, HEAD_DIM))
            st_s["a_v"].append(kvn[:, :, KV_A:].reshape(Bs, Ls, HKV_A, HEAD_DIM))
            st_s["b_c"].append(s_c)
            st_s["b_n"].append(s_n)
            st_s["b_m"].append(s_m[:, :, 0])
            st_s["b_conv"].append(conv_tail[:, TAIL_ROWS - (CONV_W - 1):, :])
        else:
            col_scale = jnp.where(jnp.arange(3 * W_C) < W_C, QK_SCALE, 1.0).astype(F32)
            w_in = (w_in_c[j] * col_scale[None, :]).astype(BF16)
            wo_parts = [w_out_c[j].astype(BF16)]
            bias = _relbias_call(relbias_c[j])
            lc = cache_c_k.shape[2]
            tail_c = NPREV_C * CHUNK

            segs_p = [
                (0, W_C, BF16, False, 0, 1),
                (W_C, 2 * W_C, BF16, True, tail_c, 2),
            ]
            qc, kvc, k_tail, v_tail = _inproj_call(xp, scp1, shp1, g0, w_in, segs_p, tm_p, 1, "inproj_c_p")
            out_c = _band_attn_call(qc, kvc, bias.reshape(H_C * CHUNK, BAND_C), wg_c, groups_c, BAND_C, False,
                                    CHUNKS_PER_STEP_C, PHASE_UNITS_C, "attn_c_p")
            xp = _outffn_call([out_c], wo_parts, xp, gtp1, scp2, shp2, gtp2, ng, ffn_w, tm_p, "outffn_c_p")
            st_p["c_k"].append(k_tail.reshape(Bp, tail_c, H_C, HEAD_DIM))
            st_p["c_v"].append(v_tail.reshape(Bp, tail_c, H_C, HEAD_DIM))

            segs_s = [
                (0, W_C, BF16, False, 0, 1),
                (W_C, 2 * W_C, F32, False, 0, 1),
            ]
            qc, kvn = _inproj_call(xs, scs1, shs1, g0, w_in, segs_s, Ls, 0, "inproj_c_s", bb=Bs)
            off = BAND_C - CHUNK - lc
            bias_c = bias[:, :Ls, off:off + lc].reshape(H_C * Ls, lc)
            bias_n = bias[:, :Ls, off + lc:off + lc + Ls].reshape(H_C * Ls, Ls)
            out_c = _cache_attn_call(qc, cache_c_k[j].reshape(Bs, lc, W_C), cache_c_v[j].reshape(Bs, lc, W_C),
                                     kvn, (bias_c, bias_n), wg_c, groups_c, False, "attn_c_s")
            xs = _outffn_call([out_c], wo_parts, xs, gts1, scs2, shs2, gts2, ng, ffn_w, Ls, "outffn_c_s", bb=Bs)
            st_s["c_k"].append(kvn[:, :, :W_C].reshape(Bs, Ls, H_C, HEAD_DIM))
            st_s["c_v"].append(kvn[:, :, W_C:].reshape(Bs, Ls, H_C, HEAD_DIM))

    order = ("a_k", "a_v", "b_c", "b_n", "b_m", "b_conv", "c_k", "c_v")
    outs = [xp, xs]
    outs += [jnp.stack(st_p[k]) for k in order]
    outs += [jnp.stack(st_s[k]) for k in order]
    return tuple(outs)
```

```python
import jax
import jax.numpy as jnp
from jax import lax
from jax.experimental import pallas as pl
from jax.experimental.pallas import tpu as pltpu

F32 = jnp.float32
BF16 = jnp.bfloat16

D_MODEL = 1024
CHUNK = 64
HEAD_DIM = 64
HQ_A = 8
HKV_A = 2
WINDOW_A = 128
H_B = 4
DH_B = 128
CONV_W = 4
H_C = 16
NPREV_C = 8
MAX_REL_C = 256
EPS = 1e-6
NEG = -1e30
W_A = HQ_A * HEAD_DIM
KV_A = HKV_A * HEAD_DIM
W_B = H_B * DH_B
W_C = H_C * HEAD_DIM
BAND_A = WINDOW_A + CHUNK
BAND_C = (NPREV_C + 1) * CHUNK
FF_CHUNK = 256
GATE_W = 128
TAIL_ROWS = 8
ROW_TILE = 512
V7X_VMEM_LIMIT = 56 * 1024 * 1024
LOG2E = 1.4426950408889634
QK_SCALE = HEAD_DIM ** -0.5 * LOG2E
PERM_A = tuple(h for pair in zip(range(HQ_A // HKV_A), range(HQ_A // HKV_A, HQ_A)) for h in pair)
CHUNKS_PER_STEP_A = 8
CHUNKS_PER_STEP_C = 4
PHASE_UNITS_A = 8
PHASE_UNITS_C = 1
MLSTM_SEQS_PER_STEP = 4
MLSTM_CHUNK = 128


def _params(sem, vmem=V7X_VMEM_LIMIT, flags=None):
    return pltpu.CompilerParams(dimension_semantics=sem, vmem_limit_bytes=vmem, flags=flags)


def _const_spec(shape):
    nd = len(shape)
    return pl.BlockSpec(shape, lambda *_: (0,) * nd, pipeline_mode=pl.Buffered(1))


def _split3(x):
    hi = x.astype(BF16)
    r1 = x - hi.astype(F32)
    mid = r1.astype(BF16)
    lo = (r1 - mid.astype(F32)).astype(BF16)
    return hi, mid, lo


def _pad_rows(x, rows):
    if x.shape[0] == rows:
        return x
    return jnp.concatenate([x, jnp.zeros((rows - x.shape[0], x.shape[1]), x.dtype)], axis=0)


def _split3_f32(x):
    return tuple(p.astype(F32) for p in _split3(x))


def _dot(a, b):
    return jnp.dot(a, b, preferred_element_type=F32)


def _dot_nt(a, b):
    return lax.dot_general(a, b, (((1,), (1,)), ((), ())), preferred_element_type=F32)


def _rms(x, g):
    return x * lax.rsqrt(jnp.mean(x * x, axis=-1, keepdims=True) + EPS) * g


def _sigmoid(x):
    return 1.0 / (1.0 + jnp.exp(-x))


def _log_sigmoid(x):
    return jnp.minimum(x, 0.0) - jnp.log(1.0 + jnp.exp(-jnp.abs(x)))


def _ada_kernel(c_ref, w_ref, b_ref, o_ref):
    c = c_ref[...]
    s = c * _sigmoid(c)
    s_hi, s_mid, _ = _split3(s)
    w_hi, w_mid, _ = _split3(w_ref[0])
    acc = _dot(s_hi, w_hi) + _dot(s_hi, w_mid) + _dot(s_mid, w_hi)
    o_ref[0] = acc + b_ref[0]


def _ada_call(c_all, w_ada, b_ada):
    depth, d, n6 = w_ada.shape
    nb = c_all.shape[0]
    tn = 512
    return pl.pallas_call(
        _ada_kernel,
        grid=(depth, n6 // tn),
        in_specs=[
            pl.BlockSpec((nb, d), lambda i, j: (0, 0)),
            pl.BlockSpec((1, d, tn), lambda i, j: (i, 0, j)),
            pl.BlockSpec((1, 1, tn), lambda i, j: (i, 0, j)),
        ],
        out_specs=pl.BlockSpec((1, nb, tn), lambda i, j: (i, 0, j)),
        out_shape=jax.ShapeDtypeStruct((depth, nb, n6), F32),
        compiler_params=_params(("arbitrary", "arbitrary")),
        name="ada_mod",
    )(c_all, w_ada, b_ada.reshape(depth, 1, n6))


def _inproj_call(x, sc, sh, g, w, segs, tm, pad, name, bb=1):
    B, L, d = x.shape
    nT = L // tm
    P = w.shape[1]
    n_seg = len(segs)

    def kernel(x_ref, sc_ref, sh_ref, g_ref, w_ref, *outs):
        seg_refs = outs[:n_seg]
        tail_refs = outs[n_seg:]
        t = pl.program_id(1)

        def compute():
            half = tm // 2 if (bb == 1 and tm % 32 == 0) else tm
            hs = []
            for r0 in range(0, tm, half):
                xs = x_ref[:, r0:r0 + half, :]
                hh = (_rms(xs, g_ref[...]) * (1.0 + sc_ref[...]) + sh_ref[...]).astype(BF16)
                hs.append(hh.reshape(bb * half, d))
            ti = 0
            for (c0, wd, dt, _, tail_rows, tail_split), r in zip(segs, seg_refs):
                ys = [_dot(hh, w_ref[:, c0:c0 + wd]) for hh in hs]
                y = (ys[0] if len(ys) == 1 else jnp.concatenate(ys, axis=0)).reshape(bb, tm, wd)
                r[...] = y.astype(dt)
                if tail_rows:
                    trs = tail_refs[ti:ti + tail_split]
                    ti += tail_split
                    ws = wd // tail_split

                    @pl.when(t == nT - 1 + pad)
                    def _():
                        for k, tr in enumerate(trs):
                            tr[...] = y[:, tm - tail_rows:, k * ws:(k + 1) * ws]

        if pad:
            @pl.when(t == 0)
            def _():
                for (_, wd, dt, padded, _, _), r in zip(segs, seg_refs):
                    if padded:
                        r[...] = jnp.zeros((bb, tm, wd), dt)

            pl.when(t > 0)(compute)
        else:
            compute()

    def row_idx(t):
        return jnp.maximum(t - 1, 0) if pad else t

    in_specs = [
        pl.BlockSpec((bb, tm, d), lambda b, t: (b, row_idx(t), 0)),
        pl.BlockSpec((bb, 1, d), lambda b, t: (b, 0, 0)),
        pl.BlockSpec((bb, 1, d), lambda b, t: (b, 0, 0)),
        _const_spec((1, d)),
        _const_spec((d, P)),
    ]
    out_specs, out_shapes = [], []
    for (_, wd, dt, padded, _, _) in segs:
        if padded:
            out_specs.append(pl.BlockSpec((bb, tm, wd), lambda b, t: (b, t, 0)))
            out_shapes.append(jax.ShapeDtypeStruct((B, tm + L, wd), dt))
        else:
            out_specs.append(pl.BlockSpec((bb, tm, wd), lambda b, t: (b, row_idx(t), 0)))
            out_shapes.append(jax.ShapeDtypeStruct((B, L, wd), dt))
    for (_, wd, _, _, tail_rows, tail_split) in segs:
        for _ in range(tail_split if tail_rows else 0):
            out_specs.append(pl.BlockSpec((bb, tail_rows, wd // tail_split), lambda b, t: (b, 0, 0)))
            out_shapes.append(jax.ShapeDtypeStruct((B, tail_rows, wd // tail_split), F32))

    return pl.pallas_call(
        kernel,
        grid=(B // bb, nT + pad),
        in_specs=in_specs,
        out_specs=out_specs,
        out_shape=out_shapes,
        compiler_params=_params(("arbitrary", "arbitrary")),
        name=name,
    )(x, sc, sh, g, w)


def _attend_grouped(q_tiles, pieces, sink_col, lq):
    return _attend_units([(q_tiles, pieces, sink_col)], lq)[0]


def _attend_units(units, lq):
    wg = units[0][0][0].shape[1]
    nb = wg // HEAD_DIM
    blk = lax.shift_right_logical(lax.broadcasted_iota(jnp.int32, (lq, wg), 1), HEAD_DIM.bit_length() - 1)
    scores = []
    for q_tiles, pieces, _ in units:
        rows = []
        for qt in q_tiles:
            for r in range(nb):
                rows.append(jnp.where(blk == r, qt, 0.0).astype(BF16))
        qbd = jnp.concatenate(rows, axis=0)
        ss = []
        for k, _, bias, valid in pieces:
            s = _dot_nt(qbd, k) + bias
            if valid is not None:
                s = jnp.where(valid, s, NEG)
            ss.append(s)
        scores.append(ss)
    maxima = []
    for ss, (_, _, sink_col) in zip(scores, units):
        m = ss[0].max(axis=-1, keepdims=True)
        for s in ss[1:]:
            m = jnp.maximum(m, s.max(axis=-1, keepdims=True))
        if sink_col is not None:
            m = jnp.maximum(m, sink_col)
        maxima.append(m)
    probs, dens = [], []
    for ss, m, (_, _, sink_col) in zip(scores, maxima, units):
        ps = [jnp.exp2(s - m) for s in ss]
        den = ps[0].sum(axis=-1, keepdims=True)
        for p in ps[1:]:
            den = den + p.sum(axis=-1, keepdims=True)
        if sink_col is not None:
            den = den + jnp.exp2(sink_col - m)
        probs.append([p.astype(BF16) for p in ps])
        dens.append(den)
    results = []
    for ps, den, (q_tiles, pieces, _) in zip(probs, dens, units):
        o = _dot(ps[0], pieces[0][1])
        for p, piece in zip(ps[1:], pieces[1:]):
            o = o + _dot(p, piece[1])
        o = o / den
        outs = []
        for t in range(len(q_tiles)):
            base = t * nb * lq
            acc = o[base + (nb - 1) * lq:base + nb * lq]
            for r in range(nb - 2, -1, -1):
                acc = jnp.where(blk == r, o[base + r * lq:base + (r + 1) * lq], acc)
            outs.append(acc)
        results.append(outs)
    return results


def _alibi_rows(sinks, lq):
    perm = jnp.array(PERM_A)
    slopes = jnp.exp2(-8.0 * jnp.arange(1, HQ_A + 1, dtype=F32) / HQ_A)
    rc = jnp.stack([-slopes[perm], sinks.astype(F32)[perm]], axis=1) * LOG2E
    return jnp.repeat(rc, lq, axis=0)


def _band_attn_a_call(q, kv, rowc, nq, name):
    B, L, wq = q.shape
    nsteps = L // (CHUNK * nq)
    band, hist = BAND_A, WINDOW_A
    kp = 2 * KV_A
    m_lanes = HQ_A * CHUNK
    n_tiles = W_A // KV_A

    def kernel(q_ref, kv_ref, e_ref, o_ref):
        step = pl.program_id(1)
        blk = lax.shift_right_logical(lax.broadcasted_iota(jnp.int32, (CHUNK, KV_A), 1), HEAD_DIM.bit_length() - 1)
        jj = lax.broadcasted_iota(jnp.int32, (band, m_lanes), 0)
        ii = lax.broadcasted_iota(jnp.int32, (band, m_lanes), 1) & (CHUNK - 1)
        bias = e_ref[0:1, :] * jnp.abs(ii + hist - jj).astype(F32)
        sink = e_ref[1:2, :]
        zeros_v = jnp.zeros((kp - band, KV_A), F32)
        zeros_p = jnp.zeros((kp - band, m_lanes), F32)

        def run(masked):
            scores, vts = [], []
            for cc in range(nq):
                c = step * nq + cc
                start = pl.multiple_of(c * CHUNK + ROW_TILE - hist, CHUNK)
                rs = slice(cc * CHUNK, (cc + 1) * CHUNK)
                rows = []
                for t in range(n_tiles):
                    qt = q_ref[0, rs, t * KV_A:(t + 1) * KV_A].astype(F32)
                    for r in range(HKV_A):
                        rows.append(jnp.where(blk == r, qt, 0.0).astype(BF16))
                qbd = jnp.concatenate(rows, axis=0)
                s = _dot_nt(kv_ref[0, pl.ds(start, band), 0:KV_A], qbd) + bias
                if masked:
                    s = jnp.where((c * CHUNK - hist + jj) >= 0, s, NEG)
                scores.append(s)
                v = kv_ref[0, pl.ds(start, band), KV_A:2 * KV_A].astype(F32)
                vts.append(jnp.concatenate([v, zeros_v], axis=0).T.astype(BF16))
            maxima = [jnp.maximum(s.max(axis=0, keepdims=True), sink) for s in scores]
            probs = []
            for s, m in zip(scores, maxima):
                p = jnp.exp2(s - m)
                den = p.sum(axis=0, keepdims=True) + jnp.exp2(sink - m)
                probs.append(jnp.concatenate([p / den, zeros_p], axis=0).astype(BF16))
            for cc, (vt, pn) in enumerate(zip(vts, probs)):
                o = _dot(vt, pn).T
                rs = slice(cc * CHUNK, (cc + 1) * CHUNK)
                for t in range(n_tiles):
                    base = t * HKV_A * CHUNK
                    acc = o[base + (HKV_A - 1) * CHUNK:base + HKV_A * CHUNK]
                    for r in range(HKV_A - 2, -1, -1):
                        acc = jnp.where(blk == r, o[base + r * CHUNK:base + (r + 1) * CHUNK], acc)
                    o_ref[0, rs, t * KV_A:(t + 1) * KV_A] = acc.astype(BF16)

        reaches_pad = step * (nq * CHUNK) < hist
        pl.when(reaches_pad)(lambda: run(True))
        pl.when(jnp.logical_not(reaches_pad))(lambda: run(False))

    return pl.pallas_call(
        kernel,
        grid=(B, nsteps),
        in_specs=[
            pl.BlockSpec((1, nq * CHUNK, wq), lambda b, c: (b, c, 0)),
            pl.BlockSpec((1, ROW_TILE + L, kv.shape[2]), lambda b, c: (b, 0, 0)),
            _const_spec(rowc.shape),
        ],
        out_specs=pl.BlockSpec((1, nq * CHUNK, wq), lambda b, c: (b, c, 0)),
        out_shape=jax.ShapeDtypeStruct((B, L, wq), BF16),
        compiler_params=_params(("arbitrary", "arbitrary")),
        name=name,
    )(q, kv, rowc)


def _band_attn_call(q, kv, extra, wg, q_groups, band, alibi, nq, phase_units, name):
    B, L, wq = q.shape
    nsteps = L // (CHUNK * nq)
    hist = band - CHUNK
    kvw = kv.shape[2]
    v_off = kvw // 2
    nb = wg // HEAD_DIM

    def kernel(q_ref, kv_ref, e_ref, o_ref):
        step = pl.program_id(1)
        jj = lax.broadcasted_iota(jnp.int32, (1, band), 1)
        if alibi:
            m_rows = len(q_groups[0][1]) * nb * CHUNK
            ii = lax.broadcasted_iota(jnp.int32, (m_rows, band), 0) & (CHUNK - 1)
            dist = jnp.abs(ii + hist - lax.broadcasted_iota(jnp.int32, (m_rows, band), 1)).astype(F32)
        rows_of = []
        bias_of = []
        for _, q_offs in q_groups:
            r0 = (q_offs[0] // HEAD_DIM) * CHUNK
            r1 = r0 + len(q_offs) * nb * CHUNK
            rows_of.append((r0, r1))
            bias_of.append(e_ref[r0:r1, 0:1] * dist if alibi else None)
        def run(masked):
            units, dests = [], []
            for cc in range(nq):
                c = step * nq + cc
                start = pl.multiple_of(c * CHUNK + ROW_TILE - hist, CHUNK)
                valid = ((c * CHUNK - hist + jj) >= 0) if masked else None
                rs = slice(cc * CHUNK, (cc + 1) * CHUNK)
                for gi, (kv0, q_offs) in enumerate(q_groups):
                    kb = kv_ref[0, pl.ds(start, band), kv0:kv0 + wg]
                    vb = kv_ref[0, pl.ds(start, band), v_off + kv0:v_off + kv0 + wg]
                    tiles = [q_ref[0, rs, o:o + wg].astype(F32) for o in q_offs]
                    r0, r1 = rows_of[gi]
                    if alibi:
                        units.append((tiles, [(kb, vb, bias_of[gi], valid)], e_ref[r0:r1, 1:2]))
                    else:
                        units.append((tiles, [(kb, vb, e_ref[r0:r1, :], valid)], None))
                    dests.append((rs, q_offs))
            for u0 in range(0, len(units), phase_units):
                for (rs, q_offs), outs in zip(dests[u0:u0 + phase_units],
                                              _attend_units(units[u0:u0 + phase_units], CHUNK)):
                    for o, res in zip(q_offs, outs):
                        o_ref[0, rs, o:o + wg] = res.astype(BF16)

        reaches_pad = step * (nq * CHUNK) < hist
        pl.when(reaches_pad)(lambda: run(True))
        pl.when(jnp.logical_not(reaches_pad))(lambda: run(False))

    return pl.pallas_call(
        kernel,
        grid=(B, nsteps),
        in_specs=[
            pl.BlockSpec((1, nq * CHUNK, wq), lambda b, c: (b, c, 0)),
            pl.BlockSpec((1, ROW_TILE + L, kvw), lambda b, c: (b, 0, 0)),
            _const_spec(extra.shape),
        ],
        out_specs=pl.BlockSpec((1, nq * CHUNK, wq), lambda b, c: (b, c, 0)),
        out_shape=jax.ShapeDtypeStruct((B, L, wq), BF16),
        compiler_params=_params(("arbitrary", "arbitrary")),
        name=name,
    )(q, kv, extra)


def _cache_attn_call(q, kc, vc, kvn, extra, wg, q_groups, alibi, name):
    B, T, wq = q.shape
    Lc = kc.shape[1]
    v_off = kvn.shape[2] // 2
    nb = wg // HEAD_DIM
    assert T & (T - 1) == 0

    def kernel(q_ref, kc_ref, vc_ref, kvn_ref, *rest):
        m_rows = len(q_groups[0][1]) * nb * T
        if alibi:
            e_ref, o_ref = rest
            ii = lax.broadcasted_iota(jnp.int32, (m_rows, Lc), 0) & (T - 1)
            dist_c = jnp.abs(ii + Lc - lax.broadcasted_iota(jnp.int32, (m_rows, Lc), 1)).astype(F32)
            ii = lax.broadcasted_iota(jnp.int32, (m_rows, T), 0) & (T - 1)
            dist_n = jnp.abs(ii - lax.broadcasted_iota(jnp.int32, (m_rows, T), 1)).astype(F32)
        else:
            bc_ref, bn_ref, o_ref = rest
        for kv0, q_offs in q_groups:
            k_c = kc_ref[0, :, kv0:kv0 + wg].astype(BF16)
            v_c = vc_ref[0, :, kv0:kv0 + wg].astype(BF16)
            k_n = kvn_ref[0, :, kv0:kv0 + wg].astype(BF16)
            v_n = kvn_ref[0, :, v_off + kv0:v_off + kv0 + wg].astype(BF16)
            tiles = [q_ref[0, :, o:o + wg].astype(F32) for o in q_offs]
            r0 = (q_offs[0] // HEAD_DIM) * T
            r1 = r0 + m_rows
            if alibi:
                slope = e_ref[r0:r1, 0:1]
                outs = _attend_grouped(tiles, [(k_c, v_c, slope * dist_c, None), (k_n, v_n, slope * dist_n, None)],
                                       e_ref[r0:r1, 1:2], T)
            else:
                outs = _attend_grouped(tiles, [(k_c, v_c, bc_ref[r0:r1, :], None),
                                               (k_n, v_n, bn_ref[r0:r1, :], None)], None, T)
            for o, res in zip(q_offs, outs):
                o_ref[0, :, o:o + wg] = res.astype(BF16)

    in_specs = [
        pl.BlockSpec((1, T, wq), lambda b: (b, 0, 0)),
        pl.BlockSpec((1, Lc, kc.shape[2]), lambda b: (b, 0, 0)),
        pl.BlockSpec((1, Lc, vc.shape[2]), lambda b: (b, 0, 0)),
        pl.BlockSpec((1, T, kvn.shape[2]), lambda b: (b, 0, 0)),
    ]
    if alibi:
        in_specs.append(_const_spec(extra.shape))
        args = (q, kc, vc, kvn, extra)
    else:
        in_specs += [_const_spec(extra[0].shape), _const_spec(extra[1].shape)]
        args = (q, kc, vc, kvn, extra[0], extra[1])
    return pl.pallas_call(
        kernel,
        grid=(B,),
        in_specs=in_specs,
        out_specs=pl.BlockSpec((1, T, wq), lambda b: (b, 0, 0)),
        out_shape=jax.ShapeDtypeStruct((B, T, wq), BF16),
        compiler_params=_params(("arbitrary",)),
        name=name,
    )(*args)


def _relbias_kernel(tab_ref, o_ref):
    nt = tab_ref.shape[1]
    hi, mid, lo = _split3(tab_ref[...])
    tt = lax.broadcasted_iota(jnp.int32, (nt, BAND_C), 0)
    jj = lax.broadcasted_iota(jnp.int32, (nt, BAND_C), 1)

    def body(i, carry):
        idx = jnp.clip(i - jj + NPREV_C * CHUNK, -MAX_REL_C, MAX_REL_C) + MAX_REL_C
        oh = jnp.where(tt == idx, 1.0, 0.0).astype(BF16)
        o_ref[i] = (_dot(hi, oh) + _dot(mid, oh) + _dot(lo, oh)) * LOG2E
        return carry

    lax.fori_loop(0, CHUNK, body, 0)


def _relbias_call(table):
    nh, nt = table.shape
    ntp = ((nt + 127) // 128) * 128
    tab = jnp.pad(table, ((0, 0), (0, ntp - nt)))
    out = pl.pallas_call(
        _relbias_kernel,
        out_shape=jax.ShapeDtypeStruct((CHUNK, nh, BAND_C), F32),
        compiler_params=_params(None),
        name="relbias",
    )(tab)
    return jnp.transpose(out, (1, 0, 2))


def _mlstm_call(qkb, vb, og, gates, conv_w, conv_b, bg, hnorm, init, Lc, name, bb=MLSTM_SEQS_PER_STEP):
    B, L, _ = qkb.shape
    nC = L // Lc
    zero_init = init is None
    k_scale = DH_B ** -0.5

    def kernel(*refs):
        if zero_init:
            (qkb_ref, vb_ref, og_ref, gt_ref, cw_ref, cb_ref, bg_ref, hn_ref,
             hb_ref, C_ref, n_ref, m_ref, tail_ref, ext_ref) = refs
        else:
            (qkb_ref, vb_ref, og_ref, gt_ref, cw_ref, cb_ref, bg_ref, hn_ref,
             C0_ref, n0_ref, m0_ref, conv0_ref,
             hb_ref, C_ref, n_ref, m_ref, tail_ref, ext_ref) = refs
        c = pl.program_id(1)

        @pl.when(c == 0)
        def _():
            if zero_init:
                C_ref[...] = jnp.zeros(C_ref.shape, F32)
                n_ref[...] = jnp.zeros(n_ref.shape, F32)
                m_ref[...] = jnp.zeros(m_ref.shape, F32)
                tail_ref[...] = jnp.zeros(tail_ref.shape, F32)
            else:
                C_ref[...] = C0_ref[...]
                n_ref[...] = n0_ref[...]
                m_ref[...] = m0_ref[...]
                tail_ref[...] = conv0_ref[...]

        r_i = lax.broadcasted_iota(jnp.int32, (Lc, Lc), 0)
        c_i = lax.broadcasted_iota(jnp.int32, (Lc, Lc), 1)
        tri = r_i >= c_i
        tri_t = r_i <= c_i
        tri_bf = jnp.where(tri, 1.0, 0.0).astype(BF16)
        r_u = lax.broadcasted_iota(jnp.int32, (GATE_W, GATE_W), 0)
        c_u = lax.broadcasted_iota(jnp.int32, (GATE_W, GATE_W), 1)
        upper_bf = jnp.where(r_u <= c_u, 1.0, 0.0).astype(BF16)
        cw = cw_ref[...]
        cb = cb_ref[...]
        bgv = bg_ref[...]
        hn = hn_ref[...]

        gbs, g_ts, qks, l_parts, t_parts = [], [], [], [], []
        for b in range(bb):
            u = qkb_ref[b].astype(F32)
            ext_ref[b, 0:TAIL_ROWS, :] = tail_ref[b]
            ext_ref[b, TAIL_ROWS:TAIL_ROWS + Lc, :] = u
            y = cb + cw[CONV_W - 1:CONV_W] * u
            for j in range(CONV_W - 1):
                y = y + cw[j:j + 1] * ext_ref[b, pl.ds(TAIL_ROWS - (CONV_W - 1) + j, Lc), :]
            tail_ref[b] = ext_ref[b, Lc:Lc + TAIL_ROWS, :]
            qks.append(y * _sigmoid(y))

            gb = gt_ref[b] + bgv
            gbs.append(gb)
            l_parts += list(_split3_f32(_log_sigmoid(gb)))
            g_t = _pad_rows(gb, GATE_W).T
            g_ts.append(g_t)
            t_parts += list(_split3_f32(_log_sigmoid(g_t[0:2 * H_B, :])))
        col_sums = _dot(tri_bf, jnp.concatenate(l_parts, axis=1).astype(BF16))
        row_sums = _dot(jnp.concatenate(t_parts, axis=0).astype(BF16), upper_bf)

        def body(b):
            gb = gbs[b]
            qk = qks[b]
            c0 = 3 * b * GATE_W
            b_col = (col_sums[:, c0:c0 + GATE_W] + col_sums[:, c0 + GATE_W:c0 + 2 * GATE_W]
                     + col_sums[:, c0 + 2 * GATE_W:c0 + 3 * GATE_W])
            r0 = 3 * b * 2 * H_B
            b_row = row_sums[r0:r0 + 2 * H_B] + row_sums[r0 + 2 * H_B:r0 + 4 * H_B] + row_sums[r0 + 4 * H_B:r0 + 6 * H_B]
            a_row_all = g_ts[b][0:H_B, :] - b_row[H_B:2 * H_B, :]

            ogv = og_ref[b].astype(F32)
            vv = vb_ref[b]
            for h in range(H_B):
                hs = slice(h * DH_B, (h + 1) * DH_B)
                qh = qk[:, hs]
                kh = qk[:, W_B + h * DH_B:W_B + (h + 1) * DH_B] * k_scale
                vh = vv[:, hs]
                qb = qh.astype(BF16)
                kb = kh.astype(BF16)
                a_row = a_row_all[h:h + 1, 0:Lc]
                bcol = b_col[:, H_B + h:H_B + h + 1]
                a_col = gb[:, h:h + 1] - bcol
                m_prev = m_ref[b, h:h + 1, 0:1]
                if Lc == DH_B:
                    brow = b_row[H_B + h:H_B + h + 1, :]
                    c_h = C_ref[b, h]
                    n_h = n_ref[b, h:h + 1, :]
                    amat_t = jnp.where(tri_t, a_col, -jnp.inf)
                    mc = jnp.maximum(amat_t.max(axis=0, keepdims=True), m_prev)
                    dm_t = jnp.exp(amat_t - mc)
                    lhs = jnp.concatenate([c_h, kh, _pad_rows(n_h, 16)], axis=0).astype(BF16)
                    qck_t = _dot_nt(lhs, qb)
                    w_t = qck_t[DH_B:DH_B + Lc] * dm_t
                    a_int = jnp.exp(m_prev - mc)
                    v_t = vh.astype(F32).T
                    num_t = a_int * qck_t[0:DH_B] + _dot(v_t.astype(BF16), w_t.astype(BF16))
                    den = a_int * qck_t[DH_B + Lc:DH_B + Lc + 1] + w_t.sum(axis=0, keepdims=True)
                    h_t = num_t / jnp.maximum(jnp.abs(den), jnp.exp(-(brow + mc)))
                    h_t = h_t * lax.rsqrt(jnp.mean(h_t * h_t, axis=0, keepdims=True) + EPS)
                    hb_ref[b, :, hs] = (h_t.T * hn[:, hs] * _sigmoid(ogv[:, hs])).astype(BF16)

                    m_end = mc[:, Lc - 1:Lc]
                    a_end = jnp.exp(m_prev - m_end)
                    w_end = jnp.exp(a_row - m_end)
                    upd = _dot(jnp.concatenate([v_t * w_end, _pad_rows(w_end, 16)], axis=0).astype(BF16), kb)
                    C_ref[b, h] = a_end * c_h + upd[0:DH_B]
                    n_ref[b, h:h + 1, :] = a_end * n_h + upd[DH_B:DH_B + 1]
                    m_ref[b, h:h + 1, :] = jnp.broadcast_to(brow[:, Lc - 1:Lc] + m_end, (1, GATE_W))
                    continue
                amat = jnp.where(tri, a_row, -jnp.inf)
                mc = jnp.maximum(amat.max(axis=-1, keepdims=True), m_prev)
                dm = jnp.exp(amat - mc)
                c_h = C_ref[b, h]
                qck = _dot_nt(qb, jnp.concatenate([c_h.astype(BF16), kb], axis=0))
                w = qck[:, DH_B:DH_B + Lc] * dm
                a_int = jnp.exp(m_prev - mc)
                num = a_int * qck[:, 0:DH_B] + _dot(w.astype(BF16), vh)
                n_h = n_ref[b, h:h + 1, :]
                den = a_int * jnp.sum(qh * n_h, axis=-1, keepdims=True) + w.sum(axis=-1, keepdims=True)
                hh = num / jnp.maximum(jnp.abs(den), jnp.exp(-(bcol + mc)))
                hh = _rms(hh, hn[:, hs]) * _sigmoid(ogv[:, hs])
                hb_ref[b, :, hs] = hh.astype(BF16)

                m_end = mc[Lc - 1:Lc, :]
                a_end = jnp.exp(m_prev - m_end)
                w_end = jnp.exp(a_col - m_end)
                vw = vh.astype(F32) * w_end
                vw_t = _pad_rows(vw, DH_B).T.astype(BF16)
                k_p = _pad_rows(kh, DH_B).astype(BF16)
                C_ref[b, h] = a_end * c_h + _dot(vw_t, k_p)
                n_ref[b, h:h + 1, :] = a_end * n_h + jnp.sum(kh * w_end, axis=0, keepdims=True)
                m_ref[b, h:h + 1, :] = jnp.broadcast_to(bcol[Lc - 1:Lc, :] + m_end, (1, GATE_W))

        for b in range(bb):
            body(b)

    w2 = 2 * W_B
    in_specs = [
        pl.BlockSpec((bb, Lc, w2), lambda i, c: (i, c, 0)),
        pl.BlockSpec((bb, Lc, W_B), lambda i, c: (i, c, 0)),
        pl.BlockSpec((bb, Lc, W_B), lambda i, c: (i, c, 0)),
        pl.BlockSpec((bb, Lc, GATE_W), lambda i, c: (i, c, 0)),
        _const_spec((CONV_W, w2)),
        _const_spec((1, w2)),
        _const_spec((1, GATE_W)),
        _const_spec((1, W_B)),
    ]
    args = [qkb, vb, og, gates, conv_w, conv_b, bg, hnorm]
    if not zero_init:
        in_specs += [
            pl.BlockSpec((bb, H_B, DH_B, DH_B), lambda i, c: (i, 0, 0, 0)),
            pl.BlockSpec((bb, H_B, DH_B), lambda i, c: (i, 0, 0)),
            pl.BlockSpec((bb, H_B, GATE_W), lambda i, c: (i, 0, 0)),
            pl.BlockSpec((bb, TAIL_ROWS, w2), lambda i, c: (i, 0, 0)),
        ]
        args += list(init)
    out_specs = [
        pl.BlockSpec((bb, Lc, W_B), lambda i, c: (i, c, 0)),
        pl.BlockSpec((bb, H_B, DH_B, DH_B), lambda i, c: (i, 0, 0, 0)),
        pl.BlockSpec((bb, H_B, DH_B), lambda i, c: (i, 0, 0)),
        pl.BlockSpec((bb, H_B, GATE_W), lambda i, c: (i, 0, 0)),
    ]
    out_shape = [
        jax.ShapeDtypeStruct((B, L, W_B), BF16),
        jax.ShapeDtypeStruct((B, H_B, DH_B, DH_B), F32),
        jax.ShapeDtypeStruct((B, H_B, DH_B), F32),
        jax.ShapeDtypeStruct((B, H_B, GATE_W), F32),
    ]
    return pl.pallas_call(
        kernel,
        grid=(B // bb, nC),
        in_specs=in_specs,
        out_specs=out_specs,
        out_shape=out_shape,
        scratch_shapes=[pltpu.VMEM((bb, TAIL_ROWS, w2), F32), pltpu.VMEM((bb, TAIL_ROWS + Lc, w2), F32)],
        compiler_params=_params(("arbitrary", "arbitrary")),
        name=name,
    )(*args)


def _outffn_call(parts, wo_parts, x, gt1, sc2, sh2, gt2, ng, ffn_w, tm, name, bb=1):
    B, L, d = x.shape
    nT = L // tm
    n_in = len(parts)
    wg, wu, wd = ffn_w
    fc = FF_CHUNK
    n_ff = wg.shape[1] // fc
    rows = bb * tm

    def kernel(*refs):
        a_refs = refs[:n_in]
        wo_refs = refs[n_in:2 * n_in]
        (x_ref, gt1_ref, sc2_ref, sh2_ref, gt2_ref, ng_ref, wg_ref, wu_ref, wd_ref,
         o_ref, h_ref, z_ref) = refs[2 * n_in:]
        half = tm // 2 if (bb == 1 and tm % 32 == 0) else tm
        for r0 in range(0, tm, half):
            rs = slice(r0, r0 + half)
            y = None
            for a_ref, wo_ref in zip(a_refs, wo_refs):
                part = _dot(a_ref[:, rs, :].reshape(bb * half, a_ref.shape[2]), wo_ref[...])
                y = part if y is None else y + part
            x1 = x_ref[:, rs, :] + gt1_ref[...] * _rms(y, ng_ref[1:2, :]).reshape(bb, half, d)
            o_ref[:, rs, :] = x1
            h = _rms(x1, ng_ref[2:3, :]) * (1.0 + sc2_ref[...]) + sh2_ref[...]
            h_ref[bb * r0:bb * (r0 + half), :] = h.reshape(bb * half, d).astype(BF16)

        for j in range(n_ff):
            cs = slice(j * fc, (j + 1) * fc)
            gpart = _dot(h_ref[...], wg_ref[:, cs])
            upart = _dot(h_ref[...], wu_ref[:, cs])
            z_ref[j] = (gpart * _sigmoid(gpart) * upart).astype(BF16)
        for r0 in range(0, tm, half):
            zr = slice(bb * r0, bb * (r0 + half))
            acc = _dot(z_ref[0, zr, :], wd_ref[0:fc, :])
            for j in range(1, n_ff):
                acc = acc + _dot(z_ref[j, zr, :], wd_ref[j * fc:(j + 1) * fc, :])
            rs = slice(r0, r0 + half)
            o_ref[:, rs, :] = o_ref[:, rs, :] + gt2_ref[...] * _rms(acc, ng_ref[3:4, :]).reshape(bb, half, d)

    mod_spec = pl.BlockSpec((bb, 1, d), lambda b, t: (b, 0, 0))
    in_specs = [pl.BlockSpec((bb, tm, p.shape[2]), lambda b, t: (b, t, 0)) for p in parts]
    in_specs += [_const_spec(w.shape) for w in wo_parts]
    in_specs += [
        pl.BlockSpec((bb, tm, d), lambda b, t: (b, t, 0)),
        mod_spec, mod_spec, mod_spec, mod_spec,
        _const_spec(ng.shape),
        _const_spec(wg.shape),
        _const_spec(wu.shape),
        _const_spec(wd.shape),
    ]
    return pl.pallas_call(
        kernel,
        grid=(B // bb, nT),
        in_specs=in_specs,
        out_specs=pl.BlockSpec((bb, tm, d), lambda b, t: (b, t, 0)),
        out_shape=jax.ShapeDtypeStruct((B, L, d), F32),
        scratch_shapes=[pltpu.VMEM((rows, d), BF16), pltpu.VMEM((n_ff, rows, fc), BF16)],
        compiler_params=_params(("arbitrary", "arbitrary")),
        name=name,
    )(*parts, *wo_parts, x, gt1, sc2, sh2, gt2, ng, wg, wu, wd)


def _perm_heads(w, axis):
    blocks = [lax.slice_in_dim(w, h * HEAD_DIM, (h + 1) * HEAD_DIM, axis=axis) for h in PERM_A]
    return jnp.concatenate(blocks, axis=axis)


def _prep_w_in_ab(w):
    o_k = W_A
    o_qk = W_A + 2 * KV_A
    o_vb = o_qk + 2 * W_B
    o_g = o_vb + W_B
    o_og = o_g + 2 * H_B
    d = w.shape[0]
    cols = [_perm_heads(w[:, :W_A], 1) * QK_SCALE, w[:, o_k:o_qk], w[:, o_qk:o_vb], w[:, o_vb:o_g],
            w[:, o_og:o_og + W_B], w[:, o_g:o_og], jnp.zeros((d, GATE_W - 2 * H_B), w.dtype)]
    return jnp.concatenate(cols, axis=1).astype(BF16)


def _prep_ffn(wg, wu, wd):
    assert wg.shape[1] % FF_CHUNK == 0
    return wg.astype(BF16), wu.astype(BF16), wd.astype(BF16)


def kernel(x_prompt, x_sample, cache_a_k, cache_a_v, state_b_c, state_b_n, state_b_m, state_b_conv, cache_c_k, cache_c_v, c_prompt, c_sample, w_in_ab, sink_a, conv_w_b, conv_b_b, b_gates_b, hnorm_b, w_out_ab, w_in_c, relbias_c, w_out_c, w_ada, b_ada, norm_g, w_ffn_gate, w_ffn_up, w_ffn_down):
    Bp, Lp, d = x_prompt.shape
    Bs, Ls, _ = x_sample.shape
    depth = w_ada.shape[0]
    tm_p = ROW_TILE
    assert Lp % ROW_TILE == 0 and Lp % (CHUNK * CHUNKS_PER_STEP_A) == 0 and Lp % (CHUNK * CHUNKS_PER_STEP_C) == 0
    assert (Bp + Bs) % 8 == 0 and Ls % 8 == 0 and Ls >= CONV_W - 1 and Bp % MLSTM_SEQS_PER_STEP == 0 and Bs % MLSTM_SEQS_PER_STEP == 0

    mods = _ada_call(jnp.concatenate([c_prompt, c_sample], axis=0), w_ada, b_ada)

    def mod_parts(i, lo, hi):
        m = mods[i, lo:hi].reshape(hi - lo, 1, 6, d)
        return [m[:, :, k, :] for k in range(6)]

    c_qa, c_kv, c_qkb = 0, W_A, W_A + 2 * KV_A
    c_vb = c_qkb + 2 * W_B
    c_og = c_vb + W_B
    c_gt = c_og + W_B
    groups_a = [(0, list(range(0, W_A, KV_A)))]
    wg_c = 4 * HEAD_DIM
    groups_c = [(o, [o]) for o in range(0, W_C, wg_c)]

    xp, xs = x_prompt, x_sample
    st_p = {k: [] for k in ("a_k", "a_v", "b_c", "b_n", "b_m", "b_conv", "c_k", "c_v")}
    st_s = {k: [] for k in st_p}

    for i in range(depth):
        j = i // 2
        ng = norm_g[i]
        g0 = ng[0:1]
        shp1, scp1, gtp1, shp2, scp2, gtp2 = mod_parts(i, 0, Bp)
        shs1, scs1, gts1, shs2, scs2, gts2 = mod_parts(i, Bp, Bp + Bs)
        ffn_w = _prep_ffn(w_ffn_gate[i], w_ffn_up[i], w_ffn_down[i])

        if i % 2 == 0:
            w_in = _prep_w_in_ab(w_in_ab[j])
            wo_parts = [_perm_heads(w_out_ab[j][:W_A], 0).astype(BF16), w_out_ab[j][W_A:].astype(BF16)]
            conv_w = conv_w_b[j]
            conv_b = conv_b_b[j].reshape(1, 2 * W_B)
            bg = jnp.pad(b_gates_b[j], (0, GATE_W - 2 * H_B)).reshape(1, GATE_W)
            hn = hnorm_b[j].reshape(1, W_B)

            segs_p = [
                (c_qa, W_A, BF16, False, 0, 1),
                (c_kv, 2 * KV_A, BF16, True, WINDOW_A, 2),
                (c_qkb, 2 * W_B, BF16, False, TAIL_ROWS, 1),
                (c_vb, W_B, BF16, False, 0, 1),
                (c_og, W_B, BF16, False, 0, 1),
                (c_gt, GATE_W, F32, False, 0, 1),
            ]
            qa, kva, qkb, vb, og, gts, k_tail, v_tail, conv_tail = _inproj_call(
                xp, scp1, shp1, g0, w_in, segs_p, tm_p, 1, "inproj_ab_p")
            out_a = _band_attn_a_call(qa, kva, _alibi_rows(sink_a[j], CHUNK).T, CHUNKS_PER_STEP_A, "attn_a_p")
            hb, s_c, s_n, s_m = _mlstm_call(qkb, vb, og, gts, conv_w, conv_b, bg, hn, None, MLSTM_CHUNK, "mlstm_p")
            xp = _outffn_call([out_a, hb], wo_parts, xp, gtp1, scp2, shp2, gtp2, ng, ffn_w, tm_p, "outffn_ab_p")
            st_p["a_k"].append(k_tail.reshape(Bp, WINDOW_A, HKV_A, HEAD_DIM))
            st_p["a_v"].append(v_tail.reshape(Bp, WINDOW_A, HKV_A, HEAD_DIM))
            st_p["b_c"].append(s_c)
            st_p["b_n"].append(s_n)
            st_p["b_m"].append(s_m[:, :, 0])
            st_p["b_conv"].append(conv_tail[:, TAIL_ROWS - (CONV_W - 1):, :])

            segs_s = [
                (c_qa, W_A, BF16, False, 0, 1),
                (c_kv, 2 * KV_A, F32, False, 0, 1),
                (c_qkb, 2 * W_B, BF16, False, TAIL_ROWS, 1),
                (c_vb, W_B, BF16, False, 0, 1),
                (c_og, W_B, BF16, False, 0, 1),
                (c_gt, GATE_W, F32, False, 0, 1),
            ]
            qa, kvn, qkb, vb, og, gts, conv_tail = _inproj_call(
                xs, scs1, shs1, g0, w_in, segs_s, Ls, 0, "inproj_ab_s", bb=Bs)
            la = cache_a_k.shape[2]
            out_a = _cache_attn_call(qa, cache_a_k[j].reshape(Bs, la, KV_A), cache_a_v[j].reshape(Bs, la, KV_A),
                                     kvn, _alibi_rows(sink_a[j], Ls), KV_A, groups_a, True, "attn_a_s")
            init = (state_b_c[j], state_b_n[j],
                    jnp.broadcast_to(state_b_m[j][:, :, None], (Bs, H_B, GATE_W)),
                    jnp.pad(state_b_conv[j], ((0, 0), (TAIL_ROWS - (CONV_W - 1), 0), (0, 0))))
            hb, s_c, s_n, s_m = _mlstm_call(qkb, vb, og, gts, conv_w, conv_b, bg, hn, init, Ls, "mlstm_s")
            xs = _outffn_call([out_a, hb], wo_parts, xs, gts1, scs2, shs2, gts2, ng, ffn_w, Ls, "outffn_ab_s", bb=Bs)
            st_s["a_k"].append(kvn[:, :, :KV_A].reshape(Bs, Ls, HKV_A, HEAD_DIM))
            st_s["a_v"].append(kvn[:, :, KV_A:].reshape(Bs, Ls, HKV_A, HEAD_DIM))
            st_s["b_c"].append(s_c)
            st_s["b_n"].append(s_n)
            st_s["b_m"].append(s_m[:, :, 0])
            st_s["b_conv"].append(conv_tail[:, TAIL_ROWS - (CONV_W - 1):, :])
        else:
            col_scale = jnp.where(jnp.arange(3 * W_C) < W_C, QK_SCALE, 1.0).astype(F32)
            w_in = (w_in_c[j] * col_scale[None, :]).astype(BF16)
            wo_parts = [w_out_c[j].astype(BF16)]
            bias = _relbias_call(relbias_c[j])
            lc = cache_c_k.shape[2]
            tail_c = NPREV_C * CHUNK

            segs_p = [
                (0, W_C, BF16, False, 0, 1),
                (W_C, 2 * W_C, BF16, True, tail_c, 2),
            ]
            qc, kvc, k_tail, v_tail = _inproj_call(xp, scp1, shp1, g0, w_in, segs_p, tm_p, 1, "inproj_c_p")
            out_c = _band_attn_call(qc, kvc, bias.reshape(H_C * CHUNK, BAND_C), wg_c, groups_c, BAND_C, False,
                                    CHUNKS_PER_STEP_C, PHASE_UNITS_C, "attn_c_p")
            xp = _outffn_call([out_c], wo_parts, xp, gtp1, scp2, shp2, gtp2, ng, ffn_w, tm_p, "outffn_c_p")
            st_p["c_k"].append(k_tail.reshape(Bp, tail_c, H_C, HEAD_DIM))
            st_p["c_v"].append(v_tail.reshape(Bp, tail_c, H_C, HEAD_DIM))

            segs_s = [
                (0, W_C, BF16, False, 0, 1),
                (W_C, 2 * W_C, F32, False, 0, 1),
            ]
            qc, kvn = _inproj_call(xs, scs1, shs1, g0, w_in, segs_s, Ls, 0, "inproj_c_s", bb=Bs)
            off = BAND_C - CHUNK - lc
            bias_c = bias[:, :Ls, off:off + lc].reshape(H_C * Ls, lc)
            bias_n = bias[:, :Ls, off + lc:off + lc + Ls].reshape(H_C * Ls, Ls)
            out_c = _cache_attn_call(qc, cache_c_k[j].reshape(Bs, lc, W_C), cache_c_v[j].reshape(Bs, lc, W_C),
                                     kvn, (bias_c, bias_n), wg_c, groups_c, False, "attn_c_s")
            xs = _outffn_call([out_c], wo_parts, xs, gts1, scs2, shs2, gts2, ng, ffn_w, Ls, "outffn_c_s", bb=Bs)
            st_s["c_k"].append(kvn[:, :, :W_C].reshape(Bs, Ls, H_C, HEAD_DIM))
            st_s["c_v"].append(kvn[:, :, W_C:].reshape(Bs, Ls, H_C, HEAD_DIM))

    order = ("a_k", "a_v", "b_c", "b_n", "b_m", "b_conv", "c_k", "c_v")
    outs = [xp, xs]
    outs += [jnp.stack(st_p[k]) for k in order]
    outs += [jnp.stack(st_s[k]) for k in order]
    return tuple(outs)
```

```python
import jax
import jax.numpy as jnp
from jax import lax
from jax.experimental import pallas as pl
from jax.experimental.pallas import tpu as pltpu

F32 = jnp.float32
BF16 = jnp.bfloat16

D_MODEL = 1024
CHUNK = 64
HEAD_DIM = 64
HQ_A = 8
HKV_A = 2
WINDOW_A = 128
H_B = 4
DH_B = 128
CONV_W = 4
H_C = 16
NPREV_C = 8
MAX_REL_C = 256
EPS = 1e-6
NEG = -1e30
W_A = HQ_A * HEAD_DIM
KV_A = HKV_A * HEAD_DIM
W_B = H_B * DH_B
W_C = H_C * HEAD_DIM
BAND_A = WINDOW_A + CHUNK
BAND_C = (NPREV_C + 1) * CHUNK
FF_CHUNK = 256
GATE_W = 128
TAIL_ROWS = 8
ROW_TILE = 512
V7X_VMEM_LIMIT = 56 * 1024 * 1024
LOG2E = 1.4426950408889634
QK_SCALE = HEAD_DIM ** -0.5 * LOG2E
PERM_A = tuple(h for pair in zip(range(HQ_A // HKV_A), range(HQ_A // HKV_A, HQ_A)) for h in pair)
CHUNKS_PER_STEP_A = 8
CHUNKS_PER_STEP_C = 4
PHASE_UNITS_A = 8
PHASE_UNITS_C = 1
MLSTM_SEQS_PER_STEP = 8
MLSTM_CHUNK = 128


def _params(sem, vmem=V7X_VMEM_LIMIT, flags=None):
    return pltpu.CompilerParams(dimension_semantics=sem, vmem_limit_bytes=vmem, flags=flags)


def _const_spec(shape):
    nd = len(shape)
    return pl.BlockSpec(shape, lambda *_: (0,) * nd, pipeline_mode=pl.Buffered(1))


def _split3(x):
    hi = x.astype(BF16)
    r1 = x - hi.astype(F32)
    mid = r1.astype(BF16)
    lo = (r1 - mid.astype(F32)).astype(BF16)
    return hi, mid, lo


def _pad_rows(x, rows):
    if x.shape[0] == rows:
        return x
    return jnp.concatenate([x, jnp.zeros((rows - x.shape[0], x.shape[1]), x.dtype)], axis=0)


def _split3_f32(x):
    return tuple(p.astype(F32) for p in _split3(x))


def _dot(a, b):
    return jnp.dot(a, b, preferred_element_type=F32)


def _dot_nt(a, b):
    return lax.dot_general(a, b, (((1,), (1,)), ((), ())), preferred_element_type=F32)


def _rms(x, g):
    return x * lax.rsqrt(jnp.mean(x * x, axis=-1, keepdims=True) + EPS) * g


def _sigmoid(x):
    return 1.0 / (1.0 + jnp.exp(-x))


def _log_sigmoid(x):
    return jnp.minimum(x, 0.0) - jnp.log(1.0 + jnp.exp(-jnp.abs(x)))


def _ada_kernel(c_ref, w_ref, b_ref, o_ref):
    c = c_ref[...]
    s = c * _sigmoid(c)
    s_hi, s_mid, _ = _split3(s)
    w_hi, w_mid, _ = _split3(w_ref[0])
    acc = _dot(s_hi, w_hi) + _dot(s_hi, w_mid) + _dot(s_mid, w_hi)
    o_ref[0] = acc + b_ref[0]


def _ada_call(c_all, w_ada, b_ada):
    depth, d, n6 = w_ada.shape
    nb = c_all.shape[0]
    tn = 512
    return pl.pallas_call(
        _ada_kernel,
        grid=(depth, n6 // tn),
        in_specs=[
            pl.BlockSpec((nb, d), lambda i, j: (0, 0)),
            pl.BlockSpec((1, d, tn), lambda i, j: (i, 0, j)),
            pl.BlockSpec((1, 1, tn), lambda i, j: (i, 0, j)),
        ],
        out_specs=pl.BlockSpec((1, nb, tn), lambda i, j: (i, 0, j)),
        out_shape=jax.ShapeDtypeStruct((depth, nb, n6), F32),
        compiler_params=_params(("arbitrary", "arbitrary")),
        name="ada_mod",
    )(c_all, w_ada, b_ada.reshape(depth, 1, n6))


def _inproj_call(x, sc, sh, g, w, segs, tm, pad, name, bb=1):
    B, L, d = x.shape
    nT = L // tm
    P = w.shape[1]
    n_seg = len(segs)

    def kernel(x_ref, sc_ref, sh_ref, g_ref, w_ref, *outs):
        seg_refs = outs[:n_seg]
        tail_refs = outs[n_seg:]
        t = pl.program_id(1)

        def compute():
            half = tm // 2 if (bb == 1 and tm % 32 == 0) else tm
            hs = []
            for r0 in range(0, tm, half):
                xs = x_ref[:, r0:r0 + half, :]
                hh = (_rms(xs, g_ref[...]) * (1.0 + sc_ref[...]) + sh_ref[...]).astype(BF16)
                hs.append(hh.reshape(bb * half, d))
            ti = 0
            for (c0, wd, dt, _, tail_rows, tail_split), r in zip(segs, seg_refs):
                ys = [_dot(hh, w_ref[:, c0:c0 + wd]) for hh in hs]
                y = (ys[0] if len(ys) == 1 else jnp.concatenate(ys, axis=0)).reshape(bb, tm, wd)
                r[...] = y.astype(dt)
                if tail_rows:
                    trs = tail_refs[ti:ti + tail_split]
                    ti += tail_split
                    ws = wd // tail_split

                    @pl.when(t == nT - 1 + pad)
                    def _():
                        for k, tr in enumerate(trs):
                            tr[...] = y[:, tm - tail_rows:, k * ws:(k + 1) * ws]

        if pad:
            @pl.when(t == 0)
            def _():
                for (_, wd, dt, padded, _, _), r in zip(segs, seg_refs):
                    if padded:
                        r[...] = jnp.zeros((bb, tm, wd), dt)

            pl.when(t > 0)(compute)
        else:
            compute()

    def row_idx(t):
        return jnp.maximum(t - 1, 0) if pad else t

    in_specs = [
        pl.BlockSpec((bb, tm, d), lambda b, t: (b, row_idx(t), 0)),
        pl.BlockSpec((bb, 1, d), lambda b, t: (b, 0, 0)),
        pl.BlockSpec((bb, 1, d), lambda b, t: (b, 0, 0)),
        _const_spec((1, d)),
        _const_spec((d, P)),
    ]
    out_specs, out_shapes = [], []
    for (_, wd, dt, padded, _, _) in segs:
        if padded:
            out_specs.append(pl.BlockSpec((bb, tm, wd), lambda b, t: (b, t, 0)))
            out_shapes.append(jax.ShapeDtypeStruct((B, tm + L, wd), dt))
        else:
            out_specs.append(pl.BlockSpec((bb, tm, wd), lambda b, t: (b, row_idx(t), 0)))
            out_shapes.append(jax.ShapeDtypeStruct((B, L, wd), dt))
    for (_, wd, _, _, tail_rows, tail_split) in segs:
        for _ in range(tail_split if tail_rows else 0):
            out_specs.append(pl.BlockSpec((bb, tail_rows, wd // tail_split), lambda b, t: (b, 0, 0)))
            out_shapes.append(jax.ShapeDtypeStruct((B, tail_rows, wd // tail_split), F32))

    return pl.pallas_call(
        kernel,
        grid=(B // bb, nT + pad),
        in_specs=in_specs,
        out_specs=out_specs,
        out_shape=out_shapes,
        compiler_params=_params(("arbitrary", "arbitrary")),
        name=name,
    )(x, sc, sh, g, w)


def _attend_grouped(q_tiles, pieces, sink_col, lq):
    return _attend_units([(q_tiles, pieces, sink_col)], lq)[0]


def _attend_units(units, lq):
    wg = units[0][0][0].shape[1]
    nb = wg // HEAD_DIM
    blk = lax.shift_right_logical(lax.broadcasted_iota(jnp.int32, (lq, wg), 1), HEAD_DIM.bit_length() - 1)
    scores = []
    for q_tiles, pieces, _ in units:
        rows = []
        for qt in q_tiles:
            for r in range(nb):
                rows.append(jnp.where(blk == r, qt, 0.0).astype(BF16))
        qbd = jnp.concatenate(rows, axis=0)
        ss = []
        for k, _, bias, valid in pieces:
            s = _dot_nt(qbd, k) + bias
            if valid is not None:
                s = jnp.where(valid, s, NEG)
            ss.append(s)
        scores.append(ss)
    maxima = []
    for ss, (_, _, sink_col) in zip(scores, units):
        m = ss[0].max(axis=-1, keepdims=True)
        for s in ss[1:]:
            m = jnp.maximum(m, s.max(axis=-1, keepdims=True))
        if sink_col is not None:
            m = jnp.maximum(m, sink_col)
        maxima.append(m)
    probs, dens = [], []
    for ss, m, (_, _, sink_col) in zip(scores, maxima, units):
        ps = [jnp.exp2(s - m) for s in ss]
        den = ps[0].sum(axis=-1, keepdims=True)
        for p in ps[1:]:
            den = den + p.sum(axis=-1, keepdims=True)
        if sink_col is not None:
            den = den + jnp.exp2(sink_col - m)
        probs.append([p.astype(BF16) for p in ps])
        dens.append(den)
    results = []
    for ps, den, (q_tiles, pieces, _) in zip(probs, dens, units):
        o = _dot(ps[0], pieces[0][1])
        for p, piece in zip(ps[1:], pieces[1:]):
            o = o + _dot(p, piece[1])
        o = o / den
        outs = []
        for t in range(len(q_tiles)):
            base = t * nb * lq
            acc = o[base + (nb - 1) * lq:base + nb * lq]
            for r in range(nb - 2, -1, -1):
                acc = jnp.where(blk == r, o[base + r * lq:base + (r + 1) * lq], acc)
            outs.append(acc)
        results.append(outs)
    return results


def _alibi_rows(sinks, lq):
    perm = jnp.array(PERM_A)
    slopes = jnp.exp2(-8.0 * jnp.arange(1, HQ_A + 1, dtype=F32) / HQ_A)
    rc = jnp.stack([-slopes[perm], sinks.astype(F32)[perm]], axis=1) * LOG2E
    return jnp.repeat(rc, lq, axis=0)


def _band_attn_a_call(q, kv, rowc, nq, name):
    B, L, wq = q.shape
    nsteps = L // (CHUNK * nq)
    band, hist = BAND_A, WINDOW_A
    kp = 2 * KV_A
    m_lanes = HQ_A * CHUNK
    n_tiles = W_A // KV_A

    def kernel(q_ref, kv_ref, e_ref, o_ref):
        step = pl.program_id(1)
        blk = lax.shift_right_logical(lax.broadcasted_iota(jnp.int32, (CHUNK, KV_A), 1), HEAD_DIM.bit_length() - 1)
        jj = lax.broadcasted_iota(jnp.int32, (band, m_lanes), 0)
        ii = lax.broadcasted_iota(jnp.int32, (band, m_lanes), 1) & (CHUNK - 1)
        bias = e_ref[0:1, :] * jnp.abs(ii + hist - jj).astype(F32)
        sink = e_ref[1:2, :]
        zeros_v = jnp.zeros((kp - band, KV_A), F32)
        zeros_p = jnp.zeros((kp - band, m_lanes), F32)

        def run(masked):
            scores, vts = [], []
            for cc in range(nq):
                c = step * nq + cc
                start = pl.multiple_of(c * CHUNK + ROW_TILE - hist, CHUNK)
                rs = slice(cc * CHUNK, (cc + 1) * CHUNK)
                rows = []
                for t in range(n_tiles):
                    qt = q_ref[0, rs, t * KV_A:(t + 1) * KV_A].astype(F32)
                    for r in range(HKV_A):
                        rows.append(jnp.where(blk == r, qt, 0.0).astype(BF16))
                qbd = jnp.concatenate(rows, axis=0)
                s = _dot_nt(kv_ref[0, pl.ds(start, band), 0:KV_A], qbd) + bias
                if masked:
                    s = jnp.where((c * CHUNK - hist + jj) >= 0, s, NEG)
                scores.append(s)
                v = kv_ref[0, pl.ds(start, band), KV_A:2 * KV_A].astype(F32)
                vts.append(jnp.concatenate([v, zeros_v], axis=0).T.astype(BF16))
            maxima = [jnp.maximum(s.max(axis=0, keepdims=True), sink) for s in scores]
            probs = []
            for s, m in zip(scores, maxima):
                p = jnp.exp2(s - m)
                den = p.sum(axis=0, keepdims=True) + jnp.exp2(sink - m)
                probs.append(jnp.concatenate([p / den, zeros_p], axis=0).astype(BF16))
            for cc, (vt, pn) in enumerate(zip(vts, probs)):
                o = _dot(vt, pn).T
                rs = slice(cc * CHUNK, (cc + 1) * CHUNK)
                for t in range(n_tiles):
                    base = t * HKV_A * CHUNK
                    acc = o[base + (HKV_A - 1) * CHUNK:base + HKV_A * CHUNK]
                    for r in range(HKV_A - 2, -1, -1):
                        acc = jnp.where(blk == r, o[base + r * CHUNK:base + (r + 1) * CHUNK], acc)
                    o_ref[0, rs, t * KV_A:(t + 1) * KV_A] = acc.astype(BF16)

        reaches_pad = step * (nq * CHUNK) < hist
        pl.when(reaches_pad)(lambda: run(True))
        pl.when(jnp.logical_not(reaches_pad))(lambda: run(False))

    return pl.pallas_call(
        kernel,
        grid=(B, nsteps),
        in_specs=[
            pl.BlockSpec((1, nq * CHUNK, wq), lambda b, c: (b, c, 0)),
            pl.BlockSpec((1, ROW_TILE + L, kv.shape[2]), lambda b, c: (b, 0, 0)),
            _const_spec(rowc.shape),
        ],
        out_specs=pl.BlockSpec((1, nq * CHUNK, wq), lambda b, c: (b, c, 0)),
        out_shape=jax.ShapeDtypeStruct((B, L, wq), BF16),
        compiler_params=_params(("arbitrary", "arbitrary")),
        name=name,
    )(q, kv, rowc)


def _band_attn_call(q, kv, extra, wg, q_groups, band, alibi, nq, phase_units, name):
    B, L, wq = q.shape
    nsteps = L // (CHUNK * nq)
    hist = band - CHUNK
    kvw = kv.shape[2]
    v_off = kvw // 2
    nb = wg // HEAD_DIM

    def kernel(q_ref, kv_ref, e_ref, o_ref):
        step = pl.program_id(1)
        jj = lax.broadcasted_iota(jnp.int32, (1, band), 1)
        if alibi:
            m_rows = len(q_groups[0][1]) * nb * CHUNK
            ii = lax.broadcasted_iota(jnp.int32, (m_rows, band), 0) & (CHUNK - 1)
            dist = jnp.abs(ii + hist - lax.broadcasted_iota(jnp.int32, (m_rows, band), 1)).astype(F32)
        rows_of = []
        bias_of = []
        for _, q_offs in q_groups:
            r0 = (q_offs[0] // HEAD_DIM) * CHUNK
            r1 = r0 + len(q_offs) * nb * CHUNK
            rows_of.append((r0, r1))
            bias_of.append(e_ref[r0:r1, 0:1] * dist if alibi else None)
        def run(masked):
            units, dests = [], []
            for cc in range(nq):
                c = step * nq + cc
                start = pl.multiple_of(c * CHUNK + ROW_TILE - hist, CHUNK)
                valid = ((c * CHUNK - hist + jj) >= 0) if masked else None
                rs = slice(cc * CHUNK, (cc + 1) * CHUNK)
                for gi, (kv0, q_offs) in enumerate(q_groups):
                    kb = kv_ref[0, pl.ds(start, band), kv0:kv0 + wg]
                    vb = kv_ref[0, pl.ds(start, band), v_off + kv0:v_off + kv0 + wg]
                    tiles = [q_ref[0, rs, o:o + wg].astype(F32) for o in q_offs]
                    r0, r1 = rows_of[gi]
                    if alibi:
                        units.append((tiles, [(kb, vb, bias_of[gi], valid)], e_ref[r0:r1, 1:2]))
                    else:
                        units.append((tiles, [(kb, vb, e_ref[r0:r1, :], valid)], None))
                    dests.append((rs, q_offs))
            for u0 in range(0, len(units), phase_units):
                for (rs, q_offs), outs in zip(dests[u0:u0 + phase_units],
                                              _attend_units(units[u0:u0 + phase_units], CHUNK)):
                    for o, res in zip(q_offs, outs):
                        o_ref[0, rs, o:o + wg] = res.astype(BF16)

        reaches_pad = step * (nq * CHUNK) < hist
        pl.when(reaches_pad)(lambda: run(True))
        pl.when(jnp.logical_not(reaches_pad))(lambda: run(False))

    return pl.pallas_call(
        kernel,
        grid=(B, nsteps),
        in_specs=[
            pl.BlockSpec((1, nq * CHUNK, wq), lambda b, c: (b, c, 0)),
            pl.BlockSpec((1, ROW_TILE + L, kvw), lambda b, c: (b, 0, 0)),
            _const_spec(extra.shape),
        ],
        out_specs=pl.BlockSpec((1, nq * CHUNK, wq), lambda b, c: (b, c, 0)),
        out_shape=jax.ShapeDtypeStruct((B, L, wq), BF16),
        compiler_params=_params(("arbitrary", "arbitrary")),
        name=name,
    )(q, kv, extra)


def _cache_attn_call(q, kc, vc, kvn, extra, wg, q_groups, alibi, name):
    B, T, wq = q.shape
    Lc = kc.shape[1]
    v_off = kvn.shape[2] // 2
    nb = wg // HEAD_DIM
    assert T & (T - 1) == 0

    def kernel(q_ref, kc_ref, vc_ref, kvn_ref, *rest):
        m_rows = len(q_groups[0][1]) * nb * T
        if alibi:
            e_ref, o_ref = rest
            ii = lax.broadcasted_iota(jnp.int32, (m_rows, Lc), 0) & (T - 1)
            dist_c = jnp.abs(ii + Lc - lax.broadcasted_iota(jnp.int32, (m_rows, Lc), 1)).astype(F32)
            ii = lax.broadcasted_iota(jnp.int32, (m_rows, T), 0) & (T - 1)
            dist_n = jnp.abs(ii - lax.broadcasted_iota(jnp.int32, (m_rows, T), 1)).astype(F32)
        else:
            bc_ref, bn_ref, o_ref = rest
        for kv0, q_offs in q_groups:
            k_c = kc_ref[0, :, kv0:kv0 + wg].astype(BF16)
            v_c = vc_ref[0, :, kv0:kv0 + wg].astype(BF16)
            k_n = kvn_ref[0, :, kv0:kv0 + wg].astype(BF16)
            v_n = kvn_ref[0, :, v_off + kv0:v_off + kv0 + wg].astype(BF16)
            tiles = [q_ref[0, :, o:o + wg].astype(F32) for o in q_offs]
            r0 = (q_offs[0] // HEAD_DIM) * T
            r1 = r0 + m_rows
            if alibi:
                slope = e_ref[r0:r1, 0:1]
                outs = _attend_grouped(tiles, [(k_c, v_c, slope * dist_c, None), (k_n, v_n, slope * dist_n, None)],
                                       e_ref[r0:r1, 1:2], T)
            else:
                outs = _attend_grouped(tiles, [(k_c, v_c, bc_ref[r0:r1, :], None),
                                               (k_n, v_n, bn_ref[r0:r1, :], None)], None, T)
            for o, res in zip(q_offs, outs):
                o_ref[0, :, o:o + wg] = res.astype(BF16)

    in_specs = [
        pl.BlockSpec((1, T, wq), lambda b: (b, 0, 0)),
        pl.BlockSpec((1, Lc, kc.shape[2]), lambda b: (b, 0, 0)),
        pl.BlockSpec((1, Lc, vc.shape[2]), lambda b: (b, 0, 0)),
        pl.BlockSpec((1, T, kvn.shape[2]), lambda b: (b, 0, 0)),
    ]
    if alibi:
        in_specs.append(_const_spec(extra.shape))
        args = (q, kc, vc, kvn, extra)
    else:
        in_specs += [_const_spec(extra[0].shape), _const_spec(extra[1].shape)]
        args = (q, kc, vc, kvn, extra[0], extra[1])
    return pl.pallas_call(
        kernel,
        grid=(B,),
        in_specs=in_specs,
        out_specs=pl.BlockSpec((1, T, wq), lambda b: (b, 0, 0)),
        out_shape=jax.ShapeDtypeStruct((B, T, wq), BF16),
        compiler_params=_params(("arbitrary",)),
        name=name,
    )(*args)


def _relbias_kernel(tab_ref, o_ref):
    nt = tab_ref.shape[1]
    hi, mid, lo = _split3(tab_ref[...])
    tt = lax.broadcasted_iota(jnp.int32, (nt, BAND_C), 0)
    jj = lax.broadcasted_iota(jnp.int32, (nt, BAND_C), 1)

    def body(i, carry):
        idx = jnp.clip(i - jj + NPREV_C * CHUNK, -MAX_REL_C, MAX_REL_C) + MAX_REL_C
        oh = jnp.where(tt == idx, 1.0, 0.0).astype(BF16)
        o_ref[i] = (_dot(hi, oh) + _dot(mid, oh) + _dot(lo, oh)) * LOG2E
        return carry

    lax.fori_loop(0, CHUNK, body, 0)


def _relbias_call(table):
    nh, nt = table.shape
    ntp = ((nt + 127) // 128) * 128
    tab = jnp.pad(table, ((0, 0), (0, ntp - nt)))
    out = pl.pallas_call(
        _relbias_kernel,
        out_shape=jax.ShapeDtypeStruct((CHUNK, nh, BAND_C), F32),
        compiler_params=_params(None),
        name="relbias",
    )(tab)
    return jnp.transpose(out, (1, 0, 2))


def _mlstm_call(qkb, vb, og, gates, conv_w, conv_b, bg, hnorm, init, Lc, name, bb=MLSTM_SEQS_PER_STEP):
    B, L, _ = qkb.shape
    nC = L // Lc
    zero_init = init is None
    k_scale = DH_B ** -0.5

    def kernel(*refs):
        if zero_init:
            (qkb_ref, vb_ref, og_ref, gt_ref, cw_ref, cb_ref, bg_ref, hn_ref,
             hb_ref, C_ref, n_ref, m_ref, tail_ref, ext_ref) = refs
        else:
            (qkb_ref, vb_ref, og_ref, gt_ref, cw_ref, cb_ref, bg_ref, hn_ref,
             C0_ref, n0_ref, m0_ref, conv0_ref,
             hb_ref, C_ref, n_ref, m_ref, tail_ref, ext_ref) = refs
        c = pl.program_id(1)

        @pl.when(c == 0)
        def _():
            if zero_init:
                C_ref[...] = jnp.zeros(C_ref.shape, F32)
                n_ref[...] = jnp.zeros(n_ref.shape, F32)
                m_ref[...] = jnp.zeros(m_ref.shape, F32)
                tail_ref[...] = jnp.zeros(tail_ref.shape, F32)
            else:
                C_ref[...] = C0_ref[...]
                n_ref[...] = n0_ref[...]
                m_ref[...] = m0_ref[...]
                tail_ref[...] = conv0_ref[...]

        r_i = lax.broadcasted_iota(jnp.int32, (Lc, Lc), 0)
        c_i = lax.broadcasted_iota(jnp.int32, (Lc, Lc), 1)
        tri = r_i >= c_i
        tri_t = r_i <= c_i
        tri_bf = jnp.where(tri, 1.0, 0.0).astype(BF16)
        r_u = lax.broadcasted_iota(jnp.int32, (GATE_W, GATE_W), 0)
        c_u = lax.broadcasted_iota(jnp.int32, (GATE_W, GATE_W), 1)
        upper_bf = jnp.where(r_u <= c_u, 1.0, 0.0).astype(BF16)
        cw = cw_ref[...]
        cb = cb_ref[...]
        bgv = bg_ref[...]
        hn = hn_ref[...]

        gbs, g_ts, qks, l_parts, t_parts = [], [], [], [], []
        for b in range(bb):
            u = qkb_ref[b].astype(F32)
            ext_ref[b, 0:TAIL_ROWS, :] = tail_ref[b]
            ext_ref[b, TAIL_ROWS:TAIL_ROWS + Lc, :] = u
            y = cb + cw[CONV_W - 1:CONV_W] * u
            for j in range(CONV_W - 1):
                y = y + cw[j:j + 1] * ext_ref[b, pl.ds(TAIL_ROWS - (CONV_W - 1) + j, Lc), :]
            tail_ref[b] = ext_ref[b, Lc:Lc + TAIL_ROWS, :]
            qks.append(y * _sigmoid(y))

            gb = gt_ref[b] + bgv
            gbs.append(gb)
            l_parts += list(_split3_f32(_log_sigmoid(gb)))
            g_t = _pad_rows(gb, GATE_W).T
            g_ts.append(g_t)
            t_parts += list(_split3_f32(_log_sigmoid(g_t[0:2 * H_B, :])))
        col_sums = _dot(tri_bf, jnp.concatenate(l_parts, axis=1).astype(BF16))
        row_sums = _dot(jnp.concatenate(t_parts, axis=0).astype(BF16), upper_bf)

        def body(b):
            gb = gbs[b]
            qk = qks[b]
            c0 = 3 * b * GATE_W
            b_col = (col_sums[:, c0:c0 + GATE_W] + col_sums[:, c0 + GATE_W:c0 + 2 * GATE_W]
                     + col_sums[:, c0 + 2 * GATE_W:c0 + 3 * GATE_W])
            r0 = 3 * b * 2 * H_B
            b_row = row_sums[r0:r0 + 2 * H_B] + row_sums[r0 + 2 * H_B:r0 + 4 * H_B] + row_sums[r0 + 4 * H_B:r0 + 6 * H_B]
            a_row_all = g_ts[b][0:H_B, :] - b_row[H_B:2 * H_B, :]

            ogv = og_ref[b].astype(F32)
            vv = vb_ref[b]
            for h in range(H_B):
                hs = slice(h * DH_B, (h + 1) * DH_B)
                qh = qk[:, hs]
                kh = qk[:, W_B + h * DH_B:W_B + (h + 1) * DH_B] * k_scale
                vh = vv[:, hs]
                qb = qh.astype(BF16)
                kb = kh.astype(BF16)
                a_row = a_row_all[h:h + 1, 0:Lc]
                bcol = b_col[:, H_B + h:H_B + h + 1]
                a_col = gb[:, h:h + 1] - bcol
                m_prev = m_ref[b, h:h + 1, 0:1]
                if Lc == DH_B:
                    brow = b_row[H_B + h:H_B + h + 1, :]
                    c_h = C_ref[b, h]
                    n_h = n_ref[b, h:h + 1, :]
                    amat_t = jnp.where(tri_t, a_col, -jnp.inf)
                    mc = jnp.maximum(amat_t.max(axis=0, keepdims=True), m_prev)
                    dm_t = jnp.exp(amat_t - mc)
                    lhs = jnp.concatenate([c_h, kh, _pad_rows(n_h, 16)], axis=0).astype(BF16)
                    qck_t = _dot_nt(lhs, qb)
                    w_t = qck_t[DH_B:DH_B + Lc] * dm_t
                    a_int = jnp.exp(m_prev - mc)
                    v_t = vh.astype(F32).T
                    num_t = a_int * qck_t[0:DH_B] + _dot(v_t.astype(BF16), w_t.astype(BF16))
                    den = a_int * qck_t[DH_B + Lc:DH_B + Lc + 1] + w_t.sum(axis=0, keepdims=True)
                    h_t = num_t / jnp.maximum(jnp.abs(den), jnp.exp(-(brow + mc)))
                    h_t = h_t * lax.rsqrt(jnp.mean(h_t * h_t, axis=0, keepdims=True) + EPS)
                    hb_ref[b, :, hs] = (h_t.T * hn[:, hs] * _sigmoid(ogv[:, hs])).astype(BF16)

                    m_end = mc[:, Lc - 1:Lc]
                    a_end = jnp.exp(m_prev - m_end)
                    w_end = jnp.exp(a_row - m_end)
                    upd = _dot(jnp.concatenate([v_t * w_end, _pad_rows(w_end, 16)], axis=0).astype(BF16), kb)
                    C_ref[b, h] = a_end * c_h + upd[0:DH_B]
                    n_ref[b, h:h + 1, :] = a_end * n_h + upd[DH_B:DH_B + 1]
                    m_ref[b, h:h + 1, :] = jnp.broadcast_to(brow[:, Lc - 1:Lc] + m_end, (1, GATE_W))
                    continue
                amat = jnp.where(tri, a_row, -jnp.inf)
                mc = jnp.maximum(amat.max(axis=-1, keepdims=True), m_prev)
                dm = jnp.exp(amat - mc)
                c_h = C_ref[b, h]
                qck = _dot_nt(qb, jnp.concatenate([c_h.astype(BF16), kb], axis=0))
                w = qck[:, DH_B:DH_B + Lc] * dm
                a_int = jnp.exp(m_prev - mc)
                num = a_int * qck[:, 0:DH_B] + _dot(w.astype(BF16), vh)
                n_h = n_ref[b, h:h + 1, :]
                den = a_int * jnp.sum(qh * n_h, axis=-1, keepdims=True) + w.sum(axis=-1, keepdims=True)
                hh = num / jnp.maximum(jnp.abs(den), jnp.exp(-(bcol + mc)))
                hh = _rms(hh, hn[:, hs]) * _sigmoid(ogv[:, hs])
                hb_ref[b, :, hs] = hh.astype(BF16)

                m_end = mc[Lc - 1:Lc, :]
                a_end = jnp.exp(m_prev - m_end)
                w_end = jnp.exp(a_col - m_end)
                vw = vh.astype(F32) * w_end
                vw_t = _pad_rows(vw, DH_B).T.astype(BF16)
                k_p = _pad_rows(kh, DH_B).astype(BF16)
                C_ref[b, h] = a_end * c_h + _dot(vw_t, k_p)
                n_ref[b, h:h + 1, :] = a_end * n_h + jnp.sum(kh * w_end, axis=0, keepdims=True)
                m_ref[b, h:h + 1, :] = jnp.broadcast_to(bcol[Lc - 1:Lc, :] + m_end, (1, GATE_W))

        for b in range(bb):
            body(b)

    w2 = 2 * W_B
    in_specs = [
        pl.BlockSpec((bb, Lc, w2), lambda i, c: (i, c, 0)),
        pl.BlockSpec((bb, Lc, W_B), lambda i, c: (i, c, 0)),
        pl.BlockSpec((bb, Lc, W_B), lambda i, c: (i, c, 0)),
        pl.BlockSpec((bb, Lc, GATE_W), lambda i, c: (i, c, 0)),
        _const_spec((CONV_W, w2)),
        _const_spec((1, w2)),
        _const_spec((1, GATE_W)),
        _const_spec((1, W_B)),
    ]
    args = [qkb, vb, og, gates, conv_w, conv_b, bg, hnorm]
    if not zero_init:
        in_specs += [
            pl.BlockSpec((bb, H_B, DH_B, DH_B), lambda i, c: (i, 0, 0, 0)),
            pl.BlockSpec((bb, H_B, DH_B), lambda i, c: (i, 0, 0)),
            pl.BlockSpec((bb, H_B, GATE_W), lambda i, c: (i, 0, 0)),
            pl.BlockSpec((bb, TAIL_ROWS, w2), lambda i, c: (i, 0, 0)),
        ]
        args += list(init)
    out_specs = [
        pl.BlockSpec((bb, Lc, W_B), lambda i, c: (i, c, 0)),
        pl.BlockSpec((bb, H_B, DH_B, DH_B), lambda i, c: (i, 0, 0, 0)),
        pl.BlockSpec((bb, H_B, DH_B), lambda i, c: (i, 0, 0)),
        pl.BlockSpec((bb, H_B, GATE_W), lambda i, c: (i, 0, 0)),
    ]
    out_shape = [
        jax.ShapeDtypeStruct((B, L, W_B), BF16),
        jax.ShapeDtypeStruct((B, H_B, DH_B, DH_B), F32),
        jax.ShapeDtypeStruct((B, H_B, DH_B), F32),
        jax.ShapeDtypeStruct((B, H_B, GATE_W), F32),
    ]
    return pl.pallas_call(
        kernel,
        grid=(B // bb, nC),
        in_specs=in_specs,
        out_specs=out_specs,
        out_shape=out_shape,
        scratch_shapes=[pltpu.VMEM((bb, TAIL_ROWS, w2), F32), pltpu.VMEM((bb, TAIL_ROWS + Lc, w2), F32)],
        compiler_params=_params(("arbitrary", "arbitrary")),
        name=name,
    )(*args)


def _outffn_call(parts, wo_parts, x, gt1, sc2, sh2, gt2, ng, ffn_w, tm, name, bb=1):
    B, L, d = x.shape
    nT = L // tm
    n_in = len(parts)
    wg, wu, wd = ffn_w
    fc = FF_CHUNK
    n_ff = wg.shape[1] // fc
    rows = bb * tm

    def kernel(*refs):
        a_refs = refs[:n_in]
        wo_refs = refs[n_in:2 * n_in]
        (x_ref, gt1_ref, sc2_ref, sh2_ref, gt2_ref, ng_ref, wg_ref, wu_ref, wd_ref,
         o_ref, h_ref, z_ref) = refs[2 * n_in:]
        half = tm // 2 if (bb == 1 and tm % 32 == 0) else tm
        for r0 in range(0, tm, half):
            rs = slice(r0, r0 + half)
            y = None
            for a_ref, wo_ref in zip(a_refs, wo_refs):
                part = _dot(a_ref[:, rs, :].reshape(bb * half, a_ref.shape[2]), wo_ref[...])
                y = part if y is None else y + part
            x1 = x_ref[:, rs, :] + gt1_ref[...] * _rms(y, ng_ref[1:2, :]).reshape(bb, half, d)
            o_ref[:, rs, :] = x1
            h = _rms(x1, ng_ref[2:3, :]) * (1.0 + sc2_ref[...]) + sh2_ref[...]
            h_ref[bb * r0:bb * (r0 + half), :] = h.reshape(bb * half, d).astype(BF16)

        for j in range(n_ff):
            cs = slice(j * fc, (j + 1) * fc)
            gpart = _dot(h_ref[...], wg_ref[:, cs])
            upart = _dot(h_ref[...], wu_ref[:, cs])
            z_ref[j] = (gpart * _sigmoid(gpart) * upart).astype(BF16)
        for r0 in range(0, tm, half):
            zr = slice(bb * r0, bb * (r0 + half))
            acc = _dot(z_ref[0, zr, :], wd_ref[0:fc, :])
            for j in range(1, n_ff):
                acc = acc + _dot(z_ref[j, zr, :], wd_ref[j * fc:(j + 1) * fc, :])
            rs = slice(r0, r0 + half)
            o_ref[:, rs, :] = o_ref[:, rs, :] + gt2_ref[...] * _rms(acc, ng_ref[3:4, :]).reshape(bb, half, d)

    mod_spec = pl.BlockSpec((bb, 1, d), lambda b, t: (b, 0, 0))
    in_specs = [pl.BlockSpec((bb, tm, p.shape[2]), lambda b, t: (b, t, 0)) for p in parts]
    in_specs += [_const_spec(w.shape) for w in wo_parts]
    in_specs += [
        pl.BlockSpec((bb, tm, d), lambda b, t: (b, t, 0)),
        mod_spec, mod_spec, mod_spec, mod_spec,
        _const_spec(ng.shape),
        _const_spec(wg.shape),
        _const_spec(wu.shape),
        _const_spec(wd.shape),
    ]
    return pl.pallas_call(
        kernel,
        grid=(B // bb, nT),
        in_specs=in_specs,
        out_specs=pl.BlockSpec((bb, tm, d), lambda b, t: (b, t, 0)),
        out_shape=jax.ShapeDtypeStruct((B, L, d), F32),
        scratch_shapes=[pltpu.VMEM((rows, d), BF16), pltpu.VMEM((n_ff, rows, fc), BF16)],
        compiler_params=_params(("arbitrary", "arbitrary")),
        name=name,
    )(*parts, *wo_parts, x, gt1, sc2, sh2, gt2, ng, wg, wu, wd)


def _perm_heads(w, axis):
    blocks = [lax.slice_in_dim(w, h * HEAD_DIM, (h + 1) * HEAD_DIM, axis=axis) for h in PERM_A]
    return jnp.concatenate(blocks, axis=axis)


def _prep_w_in_ab(w):
    o_k = W_A
    o_qk = W_A + 2 * KV_A
    o_vb = o_qk + 2 * W_B
    o_g = o_vb + W_B
    o_og = o_g + 2 * H_B
    d = w.shape[0]
    cols = [_perm_heads(w[:, :W_A], 1) * QK_SCALE, w[:, o_k:o_qk], w[:, o_qk:o_vb], w[:, o_vb:o_g],
            w[:, o_og:o_og + W_B], w[:, o_g:o_og], jnp.zeros((d, GATE_W - 2 * H_B), w.dtype)]
    return jnp.concatenate(cols, axis=1).astype(BF16)


def _prep_ffn(wg, wu, wd):
    assert wg.shape[1] % FF_CHUNK == 0
    return wg.astype(BF16), wu.astype(BF16), wd.astype(BF16)


def kernel(x_prompt, x_sample, cache_a_k, cache_a_v, state_b_c, state_b_n, state_b_m, state_b_conv, cache_c_k, cache_c_v, c_prompt, c_sample, w_in_ab, sink_a, conv_w_b, conv_b_b, b_gates_b, hnorm_b, w_out_ab, w_in_c, relbias_c, w_out_c, w_ada, b_ada, norm_g, w_ffn_gate, w_ffn_up, w_ffn_down):
    Bp, Lp, d = x_prompt.shape
    Bs, Ls, _ = x_sample.shape
    depth = w_ada.shape[0]
    tm_p = ROW_TILE
    assert Lp % ROW_TILE == 0 and Lp % (CHUNK * CHUNKS_PER_STEP_A) == 0 and Lp % (CHUNK * CHUNKS_PER_STEP_C) == 0
    assert (Bp + Bs) % 8 == 0 and Ls % 8 == 0 and Ls >= CONV_W - 1 and Bp % MLSTM_SEQS_PER_STEP == 0 and Bs % MLSTM_SEQS_PER_STEP == 0

    mods = _ada_call(jnp.concatenate([c_prompt, c_sample], axis=0), w_ada, b_ada)

    def mod_parts(i, lo, hi):
        m = mods[i, lo:hi].reshape(hi - lo, 1, 6, d)
        return [m[:, :, k, :] for k in range(6)]

    c_qa, c_kv, c_qkb = 0, W_A, W_A + 2 * KV_A
    c_vb = c_qkb + 2 * W_B
    c_og = c_vb + W_B
    c_gt = c_og + W_B
    groups_a = [(0, list(range(0, W_A, KV_A)))]
    wg_c = 4 * HEAD_DIM
    groups_c = [(o, [o]) for o in range(0, W_C, wg_c)]

    xp, xs = x_prompt, x_sample
    st_p = {k: [] for k in ("a_k", "a_v", "b_c", "b_n", "b_m", "b_conv", "c_k", "c_v")}
    st_s = {k: [] for k in st_p}

    for i in range(depth):
        j = i // 2
        ng = norm_g[i]
        g0 = ng[0:1]
        shp1, scp1, gtp1, shp2, scp2, gtp2 = mod_parts(i, 0, Bp)
        shs1, scs1, gts1, shs2, scs2, gts2 = mod_parts(i, Bp, Bp + Bs)
        ffn_w = _prep_ffn(w_ffn_gate[i], w_ffn_up[i], w_ffn_down[i])

        if i % 2 == 0:
            w_in = _prep_w_in_ab(w_in_ab[j])
            wo_parts = [_perm_heads(w_out_ab[j][:W_A], 0).astype(BF16), w_out_ab[j][W_A:].astype(BF16)]
            conv_w = conv_w_b[j]
            conv_b = conv_b_b[j].reshape(1, 2 * W_B)
            bg = jnp.pad(b_gates_b[j], (0, GATE_W - 2 * H_B)).reshape(1, GATE_W)
            hn = hnorm_b[j].reshape(1, W_B)

            segs_p = [
                (c_qa, W_A, BF16, False, 0, 1),
                (c_kv, 2 * KV_A, BF16, True, WINDOW_A, 2),
                (c_qkb, 2 * W_B, BF16, False, TAIL_ROWS, 1),
                (c_vb, W_B, BF16, False, 0, 1),
                (c_og, W_B, BF16, False, 0, 1),
                (c_gt, GATE_W, F32, False, 0, 1),
            ]
            qa, kva, qkb, vb, og, gts, k_tail, v_tail, conv_tail = _inproj_call(
                xp, scp1, shp1, g0, w_in, segs_p, tm_p, 1, "inproj_ab_p")
            out_a = _band_attn_a_call(qa, kva, _alibi_rows(sink_a[j], CHUNK).T, CHUNKS_PER_STEP_A, "attn_a_p")
            hb, s_c, s_n, s_m = _mlstm_call(qkb, vb, og, gts, conv_w, conv_b, bg, hn, None, MLSTM_CHUNK, "mlstm_p")
            xp = _outffn_call([out_a, hb], wo_parts, xp, gtp1, scp2, shp2, gtp2, ng, ffn_w, tm_p, "outffn_ab_p")
            st_p["a_k"].append(k_tail.reshape(Bp, WINDOW_A, HKV_A, HEAD_DIM))
            st_p["a_v"].append(v_tail.reshape(Bp, WINDOW_A, HKV_A, HEAD_DIM))
            st_p["b_c"].append(s_c)
            st_p["b_n"].append(s_n)
            st_p["b_m"].append(s_m[:, :, 0])
            st_p["b_conv"].append(conv_tail[:, TAIL_ROWS - (CONV_W - 1):, :])

            segs_s = [
                (c_qa, W_A, BF16, False, 0, 1),
                (c_kv, 2 * KV_A, F32, False, 0, 1),
                (c_qkb, 2 * W_B, BF16, False, TAIL_ROWS, 1),
                (c_vb, W_B, BF16, False, 0, 1),
                (c_og, W_B, BF16, False, 0, 1),
                (c_gt, GATE_W, F32, False, 0, 1),
            ]
            qa, kvn, qkb, vb, og, gts, conv_tail = _inproj_call(
                xs, scs1, shs1, g0, w_in, segs_s, Ls, 0, "inproj_ab_s", bb=Bs)
            la = cache_a_k.shape[2]
            out_a = _cache_attn_call(qa, cache_a_k[j].reshape(Bs, la, KV_A), cache_a_v[j].reshape(Bs, la, KV_A),
                                     kvn, _alibi_rows(sink_a[j], Ls), KV_A, groups_a, True, "attn_a_s")
            init = (state_b_c[j], state_b_n[j],
                    jnp.broadcast_to(state_b_m[j][:, :, None], (Bs, H_B, GATE_W)),
                    jnp.pad(state_b_conv[j], ((0, 0), (TAIL_ROWS - (CONV_W - 1), 0), (0, 0))))
            hb, s_c, s_n, s_m = _mlstm_call(qkb, vb, og, gts, conv_w, conv_b, bg, hn, init, Ls, "mlstm_s")
            xs = _outffn_call([out_a, hb], wo_parts, xs, gts1, scs2, shs2, gts2, ng, ffn_w, Ls, "outffn_ab_s", bb=Bs)
            st_s["a_k"].append(kvn[:, :, :KV_A].reshape(Bs, Ls, HKV_A, HEAD_DIM))
            st_s["a_v"].append(kvn[:, :, KV_A:].reshape(Bs, Ls, HKV_A, HEAD_DIM))
            st_s["b_c"].append(s_c)
            st_s["b_n"].append(s_n)
            st_s["b_m"].append(s_m[:, :, 0])
            st_s["b_conv"].append(conv_tail[:, TAIL_ROWS - (CONV_W - 1):, :])
        else:
            col_scale = jnp.where(jnp.arange(3 * W_C) < W_C, QK_SCALE, 1.0).astype(F32)
            w_in = (w_in_c[j] * col_scale[None, :]).astype(BF16)
            wo_parts = [w_out_c[j].astype(BF16)]
            bias = _relbias_call(relbias_c[j])
            lc = cache_c_k.shape[2]
            tail_c = NPREV_C * CHUNK

            segs_p = [
                (0, W_C, BF16, False, 0, 1),
                (W_C, 2 * W_C, BF16, True, tail_c, 2),
            ]
            qc, kvc, k_tail, v_tail = _inproj_call(xp, scp1, shp1, g0, w_in, segs_p, tm_p, 1, "inproj_c_p")
            out_c = _band_attn_call(qc, kvc, bias.reshape(H_C * CHUNK, BAND_C), wg_c, groups_c, BAND_C, False,
                                    CHUNKS_PER_STEP_C, PHASE_UNITS_C, "attn_c_p")
            xp = _outffn_call([out_c], wo_parts, xp, gtp1, scp2, shp2, gtp2, ng, ffn_w, tm_p, "outffn_c_p")
            st_p["c_k"].append(k_tail.reshape(Bp, tail_c, H_C, HEAD_DIM))
            st_p["c_v"].append(v_tail.reshape(Bp, tail_c, H_C, HEAD_DIM))

            segs_s = [
                (0, W_C, BF16, False, 0, 1),
                (W_C, 2 * W_C, F32, False, 0, 1),
            ]
            qc, kvn = _inproj_call(xs, scs1, shs1, g0, w_in, segs_s, Ls, 0, "inproj_c_s", bb=Bs)
            off = BAND_C - CHUNK - lc
            bias_c = bias[:, :Ls, off:off + lc].reshape(H_C * Ls, lc)
            bias_n = bias[:, :Ls, off + lc:off + lc + Ls].reshape(H_C * Ls, Ls)
            out_c = _cache_attn_call(qc, cache_c_k[j].reshape(Bs, lc, W_C), cache_c_v[j].reshape(Bs, lc, W_C),
                                     kvn, (bias_c, bias_n), wg_c, groups_c, False, "attn_c_s")
            xs = _outffn_call([out_c], wo_parts, xs, gts1, scs2, shs2, gts2, ng, ffn_w, Ls, "outffn_c_s", bb=Bs)
            st_s["c_k"].append(kvn[:, :, :W_C].reshape(Bs, Ls, H_C, HEAD_DIM))
            st_s["c_v"].append(kvn[:, :, W_C:].reshape(Bs, Ls, H_C, HEAD_DIM))

    order = ("a_k", "a_v", "b_c", "b_n", "b_m", "b_conv", "c_k", "c_v")
    outs = [xp, xs]
    outs += [jnp.stack(st_p[k]) for k in order]
    outs += [jnp.stack(st_s[k]) for k in order]
    return tuple(outs)
```
